```python
import math
import jax, jax.numpy as jnp
from jax import lax
import numpy as np

D_MODEL = 1024
BATCH = 8
SEQ = 2048
DEPTH = 4
DEC_BATCH = 32
DEC_SEQ = 4
PAST_LEN = 8192
PAGE_SIZE = 128

HG_HEADS = 4
HG_DK = 128
HG_DV = 128
HG_CHUNK = 64
HG_W = HG_HEADS * HG_DV
DF_HEADS = 4
DF_KV = 2
DF_HD = 64
DF_VD = 2 * DF_HD
DF_ROW = 2 * DF_HD + DF_VD
DF_W = DF_HEADS * DF_VD
NS_HEADS = 8
NS_KV = 2
NS_HD = 64
NS_W = NS_HEADS * NS_HD
CMP_BLK = 32
SEL_BLK = 64
SEL_N = 8
WINDOW = 512
Q_BLOCK = 128
EPS = 1e-6
NEG = -1e30
TINY = 1e-30
FORCE = 1e4

IN_SPLITS = (
    ('hg_q', HG_HEADS * HG_DK), ('hg_f', HG_HEADS * HG_DK), ('hg_i', HG_HEADS * HG_DV), ('hg_z', HG_W),
    ('df_q', DF_HEADS * 2 * DF_HD), ('df_k', DF_KV * 2 * DF_HD), ('df_v', DF_KV * DF_VD), ('df_z', DF_W),
    ('ns_q', NS_HEADS * NS_HD), ('ns_kvc', 2 * NS_KV * NS_HD), ('ns_kvs', 2 * NS_KV * NS_HD),
    ('ns_kvw', 2 * NS_KV * NS_HD), ('ns_g', 3 * NS_HEADS), ('ns_z', NS_W),
    ('merge', 3 * D_MODEL),
)
N_IN = sum(w for _, w in IN_SPLITS)

kernel_name = 'hybrid_hgrn2_diffattn_nsa_step'


def rmsnorm(x, g):
    xf = x.astype(jnp.float32)
    y = xf * lax.rsqrt(jnp.mean(xf * xf, axis=-1, keepdims=True) + EPS)
    return (y * g.astype(jnp.float32)).astype(x.dtype)


def alibi_slopes(n_heads):
    return jnp.asarray([2.0 ** (-8.0 * (h + 1) / n_heads) for h in range(n_heads)], dtype=jnp.float32)


def split_in(u):
    cuts = np.cumsum([w for _, w in IN_SPLITS])[:-1].tolist()
    return dict(zip([n for n, _ in IN_SPLITS], jnp.split(u, cuts, axis=-1)))


def sweep_queries(fn, qpos, *xs):
    T = qpos.shape[0]
    nb = T // Q_BLOCK if (T > Q_BLOCK and T % Q_BLOCK == 0) else 1
    if nb == 1:
        return fn(qpos, *xs)
    qb = T // nb
    def blocks(a):
        return jnp.moveaxis(a.reshape(a.shape[0], nb, qb, *a.shape[2:]), 1, 0)
    out = lax.map(lambda args: fn(*args), (qpos.reshape(nb, qb),) + tuple(blocks(a) for a in xs))
    out = jnp.moveaxis(out, 0, 1)
    return out.reshape(out.shape[0], T, *out.shape[3:])


def hgrn_scan(q, k, log_f, v, s0):
    f32 = jnp.float32
    q, k, log_f, v, s0 = (a.astype(f32) for a in (q, k, log_f, v, s0))
    B, T, H, DK = q.shape
    DV = v.shape[-1]
    C = math.gcd(T, HG_CHUNK)
    n = T // C
    def to_chunks(a):
        return jnp.moveaxis(a.reshape(B, n, C, *a.shape[2:]), 1, 0)
    tri = jnp.tril(jnp.ones((C, C), bool))
    def step(S, inp):
        qc, kc, gc, vc = inp
        G = jnp.cumsum(gc, axis=1)
        o_inter = jnp.einsum('bthk,bhkv->bthv', qc * jnp.exp(G), S)
        rel = jnp.where(tri[None, :, :, None, None], G[:, :, None] - G[:, None, :], NEG)
        att = jnp.einsum('bthk,bshk,btshk->bhts', qc, kc, jnp.exp(rel))
        o_intra = jnp.einsum('bhts,bshv->bthv', att, vc)
        g_last = G[:, -1]
        S_new = jnp.exp(g_last)[..., None] * S + jnp.einsum(
            'bshk,bshv->bhkv', kc * jnp.exp(g_last[:, None] - G), vc)
        return S_new, o_inter + o_intra
    S_T, o = lax.scan(step, s0, (to_chunks(q), to_chunks(k), to_chunks(log_f), to_chunks(v)))
    return jnp.moveaxis(o, 0, 1).reshape(B, T, H, DV), S_T


def diff_attend_block(qpos, q, k, v, kpos, slopes, lam):
    s = jnp.einsum('bqgrmd,bkgmd->bgrmqk', q, k).astype(jnp.float32) * DF_HD ** -0.5
    dist = (qpos[:, None] - kpos[None, :]).astype(jnp.float32)
    s = jnp.where(dist >= 0, s - slopes[None, :, :, None, None, None] * dist, NEG)
    p = jax.nn.softmax(s, axis=-1)
    a = p[:, :, :, 0] - lam * p[:, :, :, 1]
    return jnp.einsum('bgrqk,bkgv->bqgrv', a.astype(v.dtype), v)


def compress(rows, w):
    B, t_all = rows.shape[:2]
    nc = t_all // CMP_BLK
    blocks = rows[:, :nc * CMP_BLK].reshape(B, nc, CMP_BLK, *rows.shape[2:])
    return jnp.einsum('bnjgd,j->bngd', blocks, w)


def nsa_block(qpos, q, gates, kc, vc, cend, ks, vs, kw, vw, kwpos, slopes, q0):
    f32 = jnp.float32
    B, Q, G, R, _ = q.shape
    scale = NS_HD ** -0.5
    sl = slopes[None, :, :, None, None]
    s = jnp.einsum('bqgrd,bngd->bgrqn', q, kc).astype(f32) * scale
    dist = (qpos[:, None] - cend[None, :]).astype(f32)
    valid = dist >= 0
    s = jnp.where(valid, s - sl * dist, NEG)
    e = jnp.where(valid, jnp.exp(s - jnp.max(s, axis=-1, keepdims=True)), 0.0)
    p_cmp = e / jnp.maximum(jnp.sum(e, axis=-1, keepdims=True), TINY)
    o_cmp = jnp.einsum('bgrqn,bngd->bqgrd', p_cmp.astype(vc.dtype), vc)
    n_blk = ks.shape[2]
    nc = kc.shape[1]
    ratio = SEL_BLK // CMP_BLK
    imp = jnp.sum(p_cmp, axis=2)
    imp = jnp.pad(imp, ((0, 0), (0, 0), (0, 0), (0, n_blk * ratio - nc)))
    imp = imp.reshape(B, G, Q, n_blk, ratio).sum(-1)
    jq = (qpos // SEL_BLK)[:, None]
    j = jnp.arange(n_blk)[None, :]
    forced = ((j == 0) | (j == jq) | (j == jq - 1)).astype(f32)
    score = jnp.where(j <= jq, imp + FORCE * forced, NEG)
    top, idx = lax.top_k(score, min(SEL_N, n_blk))
    picked = top > 0.5 * NEG
    bi = jnp.arange(B)[:, None, None, None]
    gi = jnp.arange(G)[None, :, None, None]
    kg = ks[bi, gi, idx]
    vg = vs[bi, gi, idx]
    kpos = idx[..., None] * SEL_BLK + jnp.arange(SEL_BLK)
    dist = (qpos[None, None, :, None, None] - kpos).astype(f32)
    valid = (dist >= 0) & picked[..., None]
    s = jnp.einsum('bqgrd,bgqnsd->bgrqns', q, kg).astype(f32) * scale
    s = jnp.where(valid[:, :, None], s - slopes[None, :, :, None, None, None] * dist[:, :, None], NEG)
    p_sel = jax.nn.softmax(s.reshape(B, G, R, Q, -1), axis=-1).reshape(s.shape)
    o_sel = jnp.einsum('bgrqns,bgqnsd->bqgrd', p_sel.astype(vg.dtype), vg)
    start = qpos[0] - q0
    L = WINDOW + Q
    kwb = lax.dynamic_slice_in_dim(kw, start, L, axis=1)
    vwb = lax.dynamic_slice_in_dim(vw, start, L, axis=1)
    pb = lax.dynamic_slice_in_dim(kwpos, start, L, axis=0)
    dist = (qpos[:, None] - pb[None, :]).astype(f32)
    valid = (dist >= 0) & (dist <= WINDOW) & (pb[None, :] >= 0)
    s = jnp.einsum('bqgrd,bkgd->bgrqk', q, kwb).astype(f32) * scale
    s = jnp.where(valid, s - sl * dist, NEG)
    p_win = jax.nn.softmax(s, axis=-1)
    o_win = jnp.einsum('bgrqk,bkgd->bqgrd', p_win.astype(vwb.dtype), vwb)
    g = jax.nn.sigmoid(gates.astype(f32))
    o = g[..., 0:1] * o_cmp + g[..., 1:2] * o_sel + g[..., 2:3] * o_win
    return o.astype(q.dtype)


def trunk_layer(x, c, past, lw, layer_idx, P):
    B, T, _ = x.shape
    f32 = jnp.float32
    dt = x.dtype
    qpos = P + jnp.arange(T, dtype=jnp.int32)
    cond = jax.nn.silu(c) @ lw['w_cond'] + lw['b_cond']
    shift, scale, gate = jnp.split(cond, 3, axis=-1)
    h = rmsnorm(x, lw['g_pre']) * (1.0 + scale[:, None]) + shift[:, None]
    u = split_in(h @ lw['w_in'])

    lb = lw['hg_lb'].reshape(HG_HEADS, HG_DK)
    fr = u['hg_f'].astype(f32).reshape(B, T, HG_HEADS, HG_DK)
    f_gate = lb + (1.0 - lb) * jax.nn.sigmoid(fr)
    log_f = jnp.log(jnp.maximum(f_gate, TINY))
    k_hg = (1.0 - lb) * jax.nn.sigmoid(-fr)
    q_hg = jax.nn.silu(u['hg_q']).reshape(B, T, HG_HEADS, HG_DK)
    v_hg = u['hg_i'].reshape(B, T, HG_HEADS, HG_DV)
    s0 = jnp.zeros((B, HG_HEADS, HG_DK, HG_DV), f32) if past is None else past['hg']
    o_hg, s_hg = hgrn_scan(q_hg, k_hg, log_f, v_hg, s0)
    o_hg = rmsnorm(o_hg.astype(dt), lw['hg_norm']).reshape(B, T, HG_W) * jax.nn.silu(u['hg_z'])
    y_a = o_hg @ lw['w_hg_out']

    R_DF = DF_HEADS // DF_KV
    q_df = u['df_q'].reshape(B, T, DF_KV, R_DF, 2, DF_HD)
    new_diff = jnp.concatenate([u['df_k'].reshape(B, T, DF_KV, 2 * DF_HD),
                                u['df_v'].reshape(B, T, DF_KV, DF_VD)], axis=-1)
    all_diff = new_diff if past is None else jnp.concatenate([past['diff'].astype(dt), new_diff], axis=1)
    t_all = all_diff.shape[1]
    k_df = all_diff[..., :2 * DF_HD].reshape(B, t_all, DF_KV, 2, DF_HD)
    v_df = all_diff[..., 2 * DF_HD:]
    kpos = jnp.arange(t_all, dtype=jnp.int32)
    lam_init = 0.8 - 0.6 * math.exp(-0.3 * layer_idx)
    lv = lw['df_lam'].astype(f32)
    lam = jnp.exp(jnp.sum(lv[0] * lv[1])) - jnp.exp(jnp.sum(lv[2] * lv[3])) + lam_init
    df_slopes = alibi_slopes(DF_HEADS).reshape(DF_KV, R_DF)
    o_df = sweep_queries(lambda qp, qb: diff_attend_block(qp, qb, k_df, v_df, kpos, df_slopes, lam), qpos, q_df)
    o_df = rmsnorm(o_df, lw['df_norm']) * (1.0 - lam_init)
    y_b = (o_df.reshape(B, T, DF_W) * jax.nn.silu(u['df_z'])) @ lw['w_df_out']

    R_NS = NS_HEADS // NS_KV
    q_ns = u['ns_q'].reshape(B, T, NS_KV, R_NS, NS_HD)
    g_ns = u['ns_g'].reshape(B, T, NS_KV, R_NS, 3)
    kv_c = u['ns_kvc'].reshape(B, T, 2, NS_KV, NS_HD)
    kv_s = u['ns_kvs'].reshape(B, T, 2, NS_KV, NS_HD)
    kv_w = u['ns_kvw'].reshape(B, T, 2, NS_KV, NS_HD)
    new_nsa = jnp.concatenate([kv_c, kv_s], axis=2)
    all_nsa = new_nsa if past is None else jnp.concatenate([past['nsa'].astype(dt), new_nsa], axis=1)
    t_all = P + T
    cw = jax.nn.softmax(lw['ns_cmp'].astype(f32), axis=-1).astype(dt)
    k_cmp = compress(all_nsa[:, :, 0], cw[0])
    v_cmp = compress(all_nsa[:, :, 1], cw[1])
    cmp_end = jnp.arange(k_cmp.shape[1], dtype=jnp.int32) * CMP_BLK + (CMP_BLK - 1)
    n_sel_blk = -(-t_all // SEL_BLK)
    sel = jnp.pad(all_nsa[:, :, 2:], ((0, 0), (0, n_sel_blk * SEL_BLK - t_all), (0, 0), (0, 0), (0, 0)))
    sel = sel.reshape(B, n_sel_blk, SEL_BLK, 2, NS_KV, NS_HD)
    k_sel = jnp.transpose(sel[:, :, :, 0], (0, 3, 1, 2, 4))
    v_sel = jnp.transpose(sel[:, :, :, 1], (0, 3, 1, 2, 4))
    if past is None:
        win_prev = jnp.zeros((B, WINDOW, 2, NS_KV, NS_HD), dt)
        new_win = kv_w[:, T - min(WINDOW, T):]
    else:
        buf = past['win'].astype(dt)
        n_buf = buf.shape[1]
        win_prev = jnp.concatenate([jnp.zeros((B, WINDOW - n_buf, 2, NS_KV, NS_HD), dt), buf], axis=1)
        new_win = jnp.concatenate([buf, kv_w], axis=1)[:, T:]
    win_all = jnp.concatenate([win_prev, kv_w], axis=1)
    win_pos = jnp.arange(P - WINDOW, P + T, dtype=jnp.int32)
    ns_slopes = alibi_slopes(NS_HEADS).reshape(NS_KV, R_NS)
    o_ns = sweep_queries(
        lambda qp, qb, gb: nsa_block(qp, qb, gb, k_cmp, v_cmp, cmp_end, k_sel, v_sel,
                                     win_all[:, :, 0], win_all[:, :, 1], win_pos, ns_slopes, P),
        qpos, q_ns, g_ns)
    y_c = (o_ns.reshape(B, T, NS_W) * jax.nn.silu(u['ns_z'])) @ lw['w_ns_out']

    mg = jax.nn.sigmoid(u['merge'].astype(f32)).reshape(B, T, 3, D_MODEL)
    merged = (mg[:, :, 0] * y_a + mg[:, :, 1] * y_b + mg[:, :, 2] * y_c).astype(dt)
    out = merged @ lw['w_out']
    x_new = x + gate[:, None] * rmsnorm(out, lw['g_post'])
    return x_new.astype(dt), (new_diff, new_nsa, new_win, s_hg.astype(dt))


def setup_inputs(seed: int = 0) -> dict:
    key = jax.random.key(seed)
    ks = jax.random.split(key, 24)
    f32 = jnp.float32
    n_pages = PAST_LEN // PAGE_SIZE
    n_phys = (5 * DEC_BATCH * n_pages) // 4
    win_buf = min(WINDOW, PAST_LEN)
    def nrm(k, shape, s):
        return jax.random.normal(k, shape, f32) * s
    page_table = jax.random.permutation(ks[0], n_phys)[:DEC_BATCH * n_pages]
    page_table = page_table.reshape(DEC_BATCH, n_pages).astype(jnp.int32)
    return {
        'x_prompt': nrm(ks[1], (BATCH, SEQ, D_MODEL), 1.0),
        'x_sample': nrm(ks[2], (DEC_BATCH, DEC_SEQ, D_MODEL), 1.0),
        'cache_diff': nrm(ks[3], (DEPTH, n_phys, PAGE_SIZE, DF_KV, DF_ROW), 1.0),
        'cache_nsa': nrm(ks[4], (DEPTH, n_phys, PAGE_SIZE, 4, NS_KV, NS_HD), 1.0),
        'cache_nsa_win': nrm(ks[5], (DEPTH, DEC_BATCH, win_buf, 2, NS_KV, NS_HD), 1.0),
        'state_hgrn': nrm(ks[6], (DEPTH, DEC_BATCH, HG_HEADS, HG_DK, HG_DV), 0.5),
        'page_table': page_table,
        'c_prompt': nrm(ks[7], (BATCH, D_MODEL), 1.0),
        'c_sample': nrm(ks[8], (DEC_BATCH, D_MODEL), 1.0),
        'w_cond': nrm(ks[9], (DEPTH, D_MODEL, 3 * D_MODEL), 0.5 * D_MODEL ** -0.5),
        'b_cond': nrm(ks[10], (DEPTH, 3 * D_MODEL), 0.01),
        'g_pre': 1.0 + nrm(ks[11], (DEPTH, D_MODEL), 0.02),
        'g_post': 1.0 + nrm(ks[12], (DEPTH, D_MODEL), 0.02),
        'w_in': nrm(ks[13], (DEPTH, D_MODEL, N_IN), D_MODEL ** -0.5),
        'hg_lb': nrm(ks[14], (DEPTH, HG_HEADS * HG_DK), 0.1),
        'hg_norm': 1.0 + nrm(ks[15], (DEPTH, HG_DV), 0.02),
        'df_lam': nrm(ks[16], (DEPTH, 4, DF_HD), 0.1),
        'df_norm': 1.0 + nrm(ks[17], (DEPTH, DF_VD), 0.02),
        'ns_cmp': nrm(ks[18], (DEPTH, 2, CMP_BLK), 0.1),
        'w_hg_out': nrm(ks[19], (DEPTH, HG_W, D_MODEL), HG_W ** -0.5),
        'w_df_out': nrm(ks[20], (DEPTH, DF_W, D_MODEL), DF_W ** -0.5),
        'w_ns_out': nrm(ks[21], (DEPTH, NS_W, D_MODEL), NS_W ** -0.5),
        'w_out': nrm(ks[22], (DEPTH, D_MODEL, D_MODEL), D_MODEL ** -0.5),
    }


def reference(x_prompt, x_sample, cache_diff, cache_nsa, cache_nsa_win, state_hgrn, page_table,
              c_prompt, c_sample, w_cond, b_cond, g_pre, g_post, w_in, hg_lb, hg_norm,
              df_lam, df_norm, ns_cmp, w_hg_out, w_df_out, w_ns_out, w_out):
    dec_b, n_pages = page_table.shape
    past_len = n_pages * PAGE_SIZE

    def gather_pages(pool):
        rows = pool[page_table]
        return rows.reshape(dec_b, past_len, *pool.shape[2:])

    lb_w = jax.nn.softmax(hg_lb.astype(jnp.float32), axis=0)
    lower_bounds = jnp.cumsum(lb_w, axis=0) - lb_w[0]

    y_p, y_s = x_prompt, x_sample
    diff_p, diff_s, nsa_p, nsa_s, win_p, win_s, hg_p, hg_s = [], [], [], [], [], [], [], []
    for l in range(DEPTH):
        lw = {'w_cond': w_cond[l], 'b_cond': b_cond[l], 'g_pre': g_pre[l], 'g_post': g_post[l],
              'w_in': w_in[l], 'hg_lb': lower_bounds[l], 'hg_norm': hg_norm[l], 'df_lam': df_lam[l],
              'df_norm': df_norm[l], 'ns_cmp': ns_cmp[l], 'w_hg_out': w_hg_out[l],
              'w_df_out': w_df_out[l], 'w_ns_out': w_ns_out[l], 'w_out': w_out[l]}
        y_p, (dp, npr, wp, hp) = trunk_layer(y_p, c_prompt, None, lw, l, 0)
        past = {'diff': gather_pages(cache_diff[l]), 'nsa': gather_pages(cache_nsa[l]),
                'win': cache_nsa_win[l], 'hg': state_hgrn[l]}
        y_s, (ds, nss, ws, hs) = trunk_layer(y_s, c_sample, past, lw, l, past_len)
        diff_p.append(dp); nsa_p.append(npr); win_p.append(wp); hg_p.append(hp)
        diff_s.append(ds); nsa_s.append(nss); win_s.append(ws); hg_s.append(hs)
    return (y_p, y_s, jnp.stack(diff_p), jnp.stack(diff_s), jnp.stack(nsa_p), jnp.stack(nsa_s),
            jnp.stack(win_p), jnp.stack(win_s), jnp.stack(hg_p), jnp.stack(hg_s))
```

```python
import functools
import math

import numpy as np
import jax
import jax.numpy as jnp
from jax import lax
from jax.experimental import pallas as pl
from jax.experimental.pallas import tpu as pltpu

F32 = jnp.float32
BF16 = jnp.bfloat16

PAGE_SIZE = 128
HG_HEADS, HG_DK, HG_DV = 4, 128, 128
HG_CHUNK = 64
DF_HEADS, DF_KV, DF_HD = 4, 2, 64
DF_VD = 2 * DF_HD
NS_HEADS, NS_KV, NS_HD = 8, 2, 64
NS_R = NS_HEADS // NS_KV
CMP_BLK, SEL_BLK, SEL_N, WINDOW = 32, 64, 8, 512
EPS, NEG, TINY, FORCE = 1e-6, -1e30, 1e-30, 1e4
REMOVED = -3e38
EXP_CLAMP = 80.0
LANES = 128
SUB = 8
VMEM_LIMIT = 48 * 1024 * 1024

IN_SPLITS = (
    ('hg_q', 512), ('hg_f', 512), ('hg_i', 512), ('hg_z', 512),
    ('df_q', 512), ('df_k', 256), ('df_v', 256), ('df_z', 512),
    ('ns_q', 512), ('ns_kvc', 256), ('ns_kvs', 256), ('ns_kvw', 256), ('ns_g', 24), ('ns_z', 512),
    ('merge', 3072),
)

KV_DIFF, KV_NSA, KV_WIN, KV_W = 0, 512, 1024, 1280
UF_HGF, UF_NSG, UF_W = 0, 512, 640
UB_MG, UB_HGQ, UB_HGI, UB_DFQ, UB_NSQ, UB_HGZ, UB_DFZ, UB_NSZ, UB_W = (
    0, 3072, 3584, 4096, 4608, 5120, 5632, 6144, 6656)


def _column_layout():
    off, o = {}, 0
    for n, w in IN_SPLITS:
        off[n] = o
        o += w
    rng = lambda n, a, b: list(range(off[n] + a, off[n] + b))
    kv = []
    for g in range(DF_KV):
        kv += rng('df_k', g * 128, (g + 1) * 128) + rng('df_v', g * 128, (g + 1) * 128)
    kv += rng('ns_kvc', 0, 256) + rng('ns_kvs', 0, 256) + rng('ns_kvw', 0, 256)
    uf = rng('hg_f', 0, 512) + rng('ns_g', 0, 24) + [-1] * 104
    pair = [g * 256 + r * 64 + d for r in range(NS_R) for g in range(NS_KV) for d in range(NS_HD)]
    ub = (rng('merge', 0, 3072) + rng('hg_q', 0, 512) + rng('hg_i', 0, 512) + rng('df_q', 0, 512)
          + [off['ns_q'] + p for p in pair] + rng('hg_z', 0, 512) + rng('df_z', 0, 512)
          + [off['ns_z'] + p for p in pair])
    assert len(kv) == KV_W and len(uf) == UF_W and len(ub) == UB_W
    return np.asarray(kv), np.asarray(uf), np.asarray(ub), np.asarray(pair)


def _dot(a, b):
    return jnp.dot(a, b, preferred_element_type=F32)


def _dot_nt(a, b):
    return lax.dot_general(a, b, (((1,), (1,)), ((), ())), preferred_element_type=F32)


def _sigmoid(x):
    return 1.0 / (1.0 + jnp.exp(-x))


def _silu(x):
    return x * _sigmoid(x)


def _iota(shape, dim):
    return lax.broadcasted_iota(jnp.int32, shape, dim)


def _params(*sem):
    return pltpu.CompilerParams(dimension_semantics=sem, vmem_limit_bytes=VMEM_LIMIT)


def _smem():
    return pl.BlockSpec(memory_space=pltpu.SMEM)


def _cond_body(c_ref, w_ref, b_ref, o_ref):
    a = _silu(c_ref[...])
    a_hi = a.astype(BF16)
    a_lo = (a - a_hi.astype(F32)).astype(BF16)
    w = w_ref[...]
    w_hi = w.astype(BF16)
    w_lo = (w - w_hi.astype(F32)).astype(BF16)
    o_ref[...] = _dot(a_hi, w_hi) + (_dot(a_hi, w_lo) + _dot(a_lo, w_hi)) + b_ref[...]


def _cond_all(c_all, w_cond, b_cond):
    depth, d, _ = w_cond.shape
    bc = c_all.shape[0]
    return pl.pallas_call(
        _cond_body,
        out_shape=jax.ShapeDtypeStruct((depth, bc, 3 * d), F32),
        grid=(depth, 3),
        in_specs=[pl.BlockSpec((bc, d), lambda l, k: (0, 0)),
                  pl.BlockSpec((None, d, d), lambda l, k: (l, 0, k)),
                  pl.BlockSpec((None, 1, d), lambda l, k: (l, 0, k))],
        out_specs=pl.BlockSpec((None, bc, d), lambda l, k: (l, 0, k)),
        compiler_params=_params("arbitrary", "arbitrary"),
        name="cond",
    )(c_all, w_cond, b_cond.reshape(depth, 1, 3 * d))


def _inproj_body(x_ref, sh_ref, sc_ref, gp_ref, w_ref, o_ref, h_scr):
    @pl.when(pl.program_id(1) == 0)
    def _():
        x = x_ref[...]
        ms = jnp.mean(x * x, axis=-1, keepdims=True)
        y = x * lax.rsqrt(ms + EPS) * gp_ref[...]
        h_scr[...] = (y * (1.0 + sc_ref[...]) + sh_ref[...]).astype(BF16)

    o_ref[...] = _dot(h_scr[...], w_ref[...]).astype(o_ref.dtype)


def _cond_spec(cond, l, k, tm, rows_per_batch):
    d = cond.shape[-1] // 3
    if cond.ndim == 4:
        tiles = rows_per_batch // tm
        return pl.BlockSpec((None, None, 1, d), lambda i, *_: (l, i // tiles, 0, k))
    return pl.BlockSpec((None, tm, d), lambda i, *_: (l, i, k))


def _inproj(x2, cond, l, g_pre, w, tn, out_dtype, tm, rows_per_batch, name):
    rows, d = x2.shape
    n = w.shape[-1]
    return pl.pallas_call(
        _inproj_body,
        out_shape=jax.ShapeDtypeStruct((rows, n), out_dtype),
        grid=(rows // tm, n // tn),
        in_specs=[pl.BlockSpec((tm, d), lambda i, j: (i, 0)),
                  _cond_spec(cond, l, 0, tm, rows_per_batch),
                  _cond_spec(cond, l, 1, tm, rows_per_batch),
                  pl.BlockSpec((None, 1, d), lambda i, j: (l, 0, 0)),
                  pl.BlockSpec((None, d, tn), lambda i, j: (l, 0, j))],
        out_specs=pl.BlockSpec((tm, tn), lambda i, j: (i, j)),
        scratch_shapes=[pltpu.VMEM((tm, d), BF16)],
        compiler_params=_params("arbitrary", "arbitrary"),
        name=name,
    )(x2, cond, cond, g_pre, w)


def _hgrn_gates(fr, lbh):
    e = jnp.exp(-jnp.abs(fr))
    inv = 1.0 / (1.0 + e)
    pos = fr >= 0
    sg = jnp.where(pos, inv, e * inv)
    sgn = jnp.where(pos, e * inv, inv)
    fg = lbh + (1.0 - lbh) * sg
    return fg, jnp.log(jnp.maximum(fg, TINY)), (1.0 - lbh) * sgn


def _hgrn_out(o, nw, z_ref, rs, hs, a_ref):
    ms = jnp.mean(o * o, axis=-1, keepdims=True)
    on = o * lax.rsqrt(ms + EPS) * nw
    a_ref[rs, hs] = (on * _silu(z_ref[rs, hs].astype(F32))).astype(BF16)


def _hgrn_chunk_body(q_ref, v_ref, f_ref, z_ref, lb_ref, nw_ref, s0_ref, a_ref, s_out_ref, s_scr, *, C, n_chunks):
    t = pl.program_id(1)

    @pl.when(t == 0)
    def _():
        s_scr[...] = s0_ref[...]

    tri = _iota((C, C), 0) >= _iota((C, C), 1)
    tri_b = jnp.where(tri, 1.0, 0.0).astype(BF16)
    mid = C // 2
    nw = nw_ref[...]
    for h in range(HG_HEADS):
        hs = slice(h * HG_DK, (h + 1) * HG_DK)
        lbh = lb_ref[:, hs]
        S = s_scr[h]
        for c in range(n_chunks):
            rs = slice(c * C, (c + 1) * C)
            _, g, k = _hgrn_gates(f_ref[rs, hs], lbh)
            q = _silu(q_ref[rs, hs].astype(F32))
            v = v_ref[rs, hs]
            g1 = g.astype(BF16)
            r1 = g - g1.astype(F32)
            g2 = r1.astype(BF16)
            g3 = (r1 - g2.astype(F32)).astype(BF16)
            G = _dot(tri_b, g1) + _dot(tri_b, g2) + _dot(tri_b, g3)
            g_mid = G[mid - 1:mid, :]
            g_last = G[C - 1:C, :]
            qe = q * jnp.exp(jnp.minimum(G - g_mid, EXP_CLAMP))
            ke = k * jnp.exp(jnp.minimum(g_mid - G, EXP_CLAMP))
            att = jnp.where(tri, _dot_nt(qe.astype(BF16), ke.astype(BF16)), 0.0)
            o = _dot((q * jnp.exp(G)).astype(BF16), S.astype(BF16)) + _dot(att.astype(BF16), v)
            _hgrn_out(o, nw, z_ref, rs, hs, a_ref)
            kd = k * jnp.exp(g_last - G)
            tile = jnp.concatenate(
                [kd, jnp.broadcast_to(jnp.exp(g_last), (SUB, HG_DK)),
                 jnp.zeros((LANES - C - SUB, HG_DK), F32)], axis=0)
            tile_t = tile.T
            S = tile_t[:, C:C + 1] * S + _dot(tile_t[:, :C].astype(BF16), v)
        s_scr[h] = S

    @pl.when(t == pl.num_programs(1) - 1)
    def _():
        s_out_ref[...] = s_scr[...]


def _hgrn_prompt(ub, uf, lb, nw, s0, batch, seq):
    ts = min(256, seq)
    C = math.gcd(ts, HG_CHUNK)
    nt = seq // ts
    row = lambda b, t: b * nt + t
    return pl.pallas_call(
        functools.partial(_hgrn_chunk_body, C=C, n_chunks=ts // C),
        out_shape=(jax.ShapeDtypeStruct((batch * seq, 512), BF16),
                   jax.ShapeDtypeStruct((batch, HG_HEADS, HG_DK, HG_DV), F32)),
        grid=(batch, nt),
        in_specs=[pl.BlockSpec((ts, 512), lambda b, t: (row(b, t), UB_HGQ // 512)),
                  pl.BlockSpec((ts, 512), lambda b, t: (row(b, t), UB_HGI // 512)),
                  pl.BlockSpec((ts, 512), lambda b, t: (row(b, t), UF_HGF // 512)),
                  pl.BlockSpec((ts, 512), lambda b, t: (row(b, t), UB_HGZ // 512)),
                  pl.BlockSpec((1, 512), lambda b, t: (0, 0)),
                  pl.BlockSpec((1, HG_DV), lambda b, t: (0, 0)),
                  pl.BlockSpec((None, HG_HEADS, HG_DK, HG_DV), lambda b, t: (b, 0, 0, 0))],
        out_specs=(pl.BlockSpec((ts, 512), lambda b, t: (row(b, t), 0)),
                   pl.BlockSpec((None, HG_HEADS, HG_DK, HG_DV), lambda b, t: (b, 0, 0, 0))),
        scratch_shapes=[pltpu.VMEM((HG_HEADS, HG_DK, HG_DV), F32)],
        compiler_params=_params("arbitrary", "arbitrary"),
        name="hgrn_prompt",
    )(ub, ub, uf, ub, lb, nw, s0)


def _hgrn_step_body(q_ref, v_ref, f_ref, z_ref, lb_ref, nw_ref, s0_ref, a_ref, s_out_ref, *, n_tok):
    rows = _iota((SUB, HG_DV), 0)
    nw = nw_ref[...]
    rs = slice(0, SUB)
    for h in range(HG_HEADS):
        hs = slice(h * HG_DK, (h + 1) * HG_DK)
        fg, _, k = _hgrn_gates(f_ref[:, hs], lb_ref[:, hs])
        q = _silu(q_ref[:, hs].astype(F32))
        v = v_ref[:, hs].astype(F32)
        tile = jnp.concatenate([fg, k, q, jnp.zeros((LANES - 3 * SUB, HG_DK), F32)], axis=0)
        tile_t = tile.T
        S = s0_ref[h]
        o = jnp.zeros((SUB, HG_DV), F32)
        for t in range(n_tok):
            S = S * tile_t[:, t:t + 1] + tile_t[:, SUB + t:SUB + t + 1] * v[t:t + 1, :]
            o_t = jnp.sum(S * tile_t[:, 2 * SUB + t:2 * SUB + t + 1], axis=0, keepdims=True)
            o = jnp.where(rows == t, o_t, o)
        s_out_ref[h] = S
        _hgrn_out(o, nw, z_ref, rs, hs, a_ref)


def _hgrn_sample(ub, uf, lb, nw, s0, batch, n_tok):
    return pl.pallas_call(
        functools.partial(_hgrn_step_body, n_tok=n_tok),
        out_shape=(jax.ShapeDtypeStruct((batch * SUB, 512), BF16),
                   jax.ShapeDtypeStruct((batch, HG_HEADS, HG_DK, HG_DV), F32)),
        grid=(batch,),
        in_specs=[pl.BlockSpec((SUB, 512), lambda b: (b, UB_HGQ // 512)),
                  pl.BlockSpec((SUB, 512), lambda b: (b, UB_HGI // 512)),
                  pl.BlockSpec((SUB, 512), lambda b: (b, UF_HGF // 512)),
                  pl.BlockSpec((SUB, 512), lambda b: (b, UB_HGZ // 512)),
                  pl.BlockSpec((1, 512), lambda b: (0, 0)),
                  pl.BlockSpec((1, HG_DV), lambda b: (0, 0)),
                  pl.BlockSpec((None, HG_HEADS, HG_DK, HG_DV), lambda b: (b, 0, 0, 0))],
        out_specs=(pl.BlockSpec((SUB, 512), lambda b: (b, 0)),
                   pl.BlockSpec((None, HG_HEADS, HG_DK, HG_DV), lambda b: (b, 0, 0, 0))),
        compiler_params=_params("arbitrary"),
        name="hgrn_sample",
    )(ub, ub, uf, ub, lb, nw, s0)


def _softmax_update(carry, s, v_fn):
    m, l, acc = carry
    m_new = jnp.maximum(m, jnp.max(s, axis=-1, keepdims=True))
    alpha = jnp.exp(m - m_new)
    p = jnp.exp(s - m_new)
    return m_new, alpha * l + jnp.sum(p, axis=-1, keepdims=True), alpha * acc + v_fn(p.astype(BF16))


def _softmax_init(rows, width=LANES):
    return (jnp.full((rows, 1), NEG, F32), jnp.zeros((rows, 1), F32), jnp.zeros((rows, width), F32))


def _half_mask(x, lane, upper):
    keep = (lane >= 64) if upper else (lane < 64)
    return jnp.where(keep, x, jnp.zeros_like(x))


def _scaled_q(q, scale):
    return (q.astype(F32) * scale).astype(BF16)


def _diff_finish(o, lam, oml, nw, z, r_rows):
    od = o[:r_rows] - lam * o[r_rows:]
    ms = jnp.mean(od * od, axis=-1, keepdims=True)
    return (od * lax.rsqrt(ms + EPS) * nw * oml * _silu(z.astype(F32))).astype(BF16)


def _diff_prompt_body(sc_ref, q_ref, kv_ref, z_ref, nw_ref, a_ref, kb, vb, *, tq):
    g = pl.program_id(1)
    i = pl.program_id(2)

    @pl.when(i == 0)
    def _():
        kb[...] = kv_ref[:, 0:128].astype(BF16)
        vb[...] = kv_ref[:, 128:256].astype(BF16)

    lam, oml = sc_ref[0], sc_ref[1]
    lane = _iota((tq, LANES), 1)
    rel = (_iota((tq, tq), 0) - _iota((tq, tq), 1)).astype(F32)
    rel2 = jnp.concatenate([rel, rel], axis=0)
    for r in range(DF_HEADS // DF_KV):
        slope = sc_ref[2 + g * (DF_HEADS // DF_KV) + r]
        qs = _scaled_q(q_ref[:, r * 128:(r + 1) * 128], DF_HD ** -0.5)
        q2 = jnp.concatenate([_half_mask(qs, lane, False), _half_mask(qs, lane, True)], axis=0)

        def tile(kt, carry, diag):
            off = pl.multiple_of(kt * tq, tq)
            k = kb[pl.ds(off, tq), :]
            v = vb[pl.ds(off, tq), :]
            dist = rel2 + ((i - kt) * tq).astype(F32)
            s = _dot_nt(q2, k) - slope * dist
            if diag:
                s = jnp.where(rel2 >= 0, s, NEG)
            return _softmax_update(carry, s, lambda p: _dot(p, v))

        carry = lax.fori_loop(0, i, lambda kt, c: tile(kt, c, False), _softmax_init(2 * tq))
        _, l, acc = tile(i, carry, True)
        a_ref[:, r * 128:(r + 1) * 128] = _diff_finish(
            acc / l, lam, oml, nw_ref[...], z_ref[:, r * 128:(r + 1) * 128], tq)


def _diff_prompt(sc, ub, ukv, nw, batch, seq):
    tq = min(256, seq)
    nq = seq // tq
    row = lambda b, g, i: b * nq + i
    return pl.pallas_call(
        functools.partial(_diff_prompt_body, tq=tq),
        out_shape=jax.ShapeDtypeStruct((batch * seq, 512), BF16),
        grid=(batch, DF_KV, nq),
        in_specs=[_smem(),
                  pl.BlockSpec((tq, 256), lambda b, g, i: (row(b, g, i), UB_DFQ // 256 + g)),
                  pl.BlockSpec((seq, 256), lambda b, g, i: (b, KV_DIFF // 256 + g)),
                  pl.BlockSpec((tq, 256), lambda b, g, i: (row(b, g, i), UB_DFZ // 256 + g)),
                  pl.BlockSpec((1, DF_VD), lambda b, g, i: (0, 0))],
        out_specs=pl.BlockSpec((tq, 256), lambda b, g, i: (row(b, g, i), g)),
        scratch_shapes=[pltpu.VMEM((seq, 128), BF16), pltpu.VMEM((seq, 128), BF16)],
        compiler_params=_params("arbitrary", "arbitrary", "arbitrary"),
        name="diff_prompt",
    )(sc, ub, ukv, ub, nw)


def _diff_sample_body(pt_ref, sc_ref, q_ref, new_ref, z_ref, nw_ref, *rest, PP, past, n_tok):
    del pt_ref
    pages = rest[:PP]
    a_ref = rest[PP]
    m_s, l_s, acc_s = rest[PP + 1:]
    c = pl.program_id(1)
    R = DF_HEADS // DF_KV
    rows = 2 * R * SUB

    @pl.when(c == 0)
    def _():
        m_s[...] = jnp.full(m_s.shape, NEG, F32)
        l_s[...] = jnp.zeros(l_s.shape, F32)
        acc_s[...] = jnp.zeros(acc_s.shape, F32)

    lane = _iota((SUB, LANES), 1)
    ridx = _iota((rows, 1), 0)
    tok = ridx & (SUB - 1)
    head = (ridx >> 3) & (R - 1)
    qpos = (past + tok).astype(F32)

    def make_q(g):
        pieces = []
        for m in range(2):
            for r in range(R):
                qs = _scaled_q(q_ref[:, g * 256 + r * 128:g * 256 + (r + 1) * 128], DF_HD ** -0.5)
                pieces.append(_half_mask(qs, lane, m == 1))
        slope = jnp.where(head == 0, sc_ref[2 + g * R], sc_ref[2 + g * R + 1])
        return jnp.concatenate(pieces, axis=0), slope

    for g in range(DF_KV):
        qg, slope = make_q(g)
        ks = g * 256
        s = jnp.concatenate([_dot_nt(qg, pages[p][:, ks:ks + 128].astype(BF16)) for p in range(PP)], axis=1)
        kpos = (c * (PP * PAGE_SIZE) + _iota((rows, PP * PAGE_SIZE), 1)).astype(F32)
        s = s - slope * (qpos - kpos)

        def pv(p):
            out = _dot(p[:, 0:PAGE_SIZE], pages[0][:, ks + 128:ks + 256].astype(BF16))
            for j in range(1, PP):
                out += _dot(p[:, j * PAGE_SIZE:(j + 1) * PAGE_SIZE], pages[j][:, ks + 128:ks + 256].astype(BF16))
            return out

        m, l, acc = _softmax_update((m_s[g], l_s[g], acc_s[g]), s, pv)
        m_s[g], l_s[g], acc_s[g] = m, l, acc

    @pl.when(c == pl.num_programs(1) - 1)
    def _():
        lam, oml = sc_ref[0], sc_ref[1]
        pad = jnp.zeros((LANES - SUB, LANES), F32)
        jcol = _iota((rows, LANES), 1)
        for g in range(DF_KV):
            qg, slope = make_q(g)
            ks = g * 256
            knew = jnp.concatenate([new_ref[:, ks:ks + 128], pad], axis=0).astype(BF16)
            vnew = jnp.concatenate([new_ref[:, ks + 128:ks + 256], pad], axis=0).astype(BF16)
            dist = (tok - jcol).astype(F32)
            s = jnp.where((jcol <= tok) & (jcol < n_tok), _dot_nt(qg, knew) - slope * dist, NEG)
            _, l, acc = _softmax_update((m_s[g], l_s[g], acc_s[g]), s, lambda p: _dot(p, vnew))
            o = acc / l
            for r in range(R):
                o_r = jnp.concatenate([o[r * SUB:(r + 1) * SUB], o[(R + r) * SUB:(R + r + 1) * SUB]], axis=0)
                cs = slice(g * 256 + r * 128, g * 256 + (r + 1) * 128)
                a_ref[:, cs] = _diff_finish(o_r, lam, oml, nw_ref[...], z_ref[:, cs], SUB)


def _diff_sample(l, sc, pt_flat, ub, ukv, nw, cache, batch, n_pages, n_phys, n_tok):
    PP = min(16, n_pages)
    past = n_pages * PAGE_SIZE
    R = DF_HEADS // DF_KV
    rows = 2 * R * SUB

    def page_spec(p):
        return pl.BlockSpec((None, PAGE_SIZE, 512),
                            lambda b, c, pt: (l * n_phys + pt[b * n_pages + c * PP + p], 0, 0))

    grid_spec = pltpu.PrefetchScalarGridSpec(
        num_scalar_prefetch=1,
        grid=(batch, n_pages // PP),
        in_specs=[_smem(),
                  pl.BlockSpec((SUB, 512), lambda b, c, pt: (b, UB_DFQ // 512)),
                  pl.BlockSpec((SUB, 512), lambda b, c, pt: (b, KV_DIFF // 512)),
                  pl.BlockSpec((SUB, 512), lambda b, c, pt: (b, UB_DFZ // 512)),
                  pl.BlockSpec((1, DF_VD), lambda b, c, pt: (0, 0))] + [page_spec(p) for p in range(PP)],
        out_specs=pl.BlockSpec((SUB, 512), lambda b, c, pt: (b, 0)),
        scratch_shapes=[pltpu.VMEM((DF_KV, rows, 1), F32), pltpu.VMEM((DF_KV, rows, 1), F32),
                        pltpu.VMEM((DF_KV, rows, DF_VD), F32)])
    return pl.pallas_call(
        functools.partial(_diff_sample_body, PP=PP, past=past, n_tok=n_tok),
        out_shape=jax.ShapeDtypeStruct((batch * SUB, 512), BF16),
        grid_spec=grid_spec,
        compiler_params=_params("arbitrary", "arbitrary"),
        name="diff_sample",
    )(pt_flat, sc, ub, ukv, ub, nw, *([cache] * PP))


def _block_sum(rows_ref, r0, cw):
    return jnp.sum(rows_ref[r0:r0 + CMP_BLK, 0:256] * cw, axis=0, keepdims=True)


def _cmp_attend(qm, slope, qposf, rowq, lane, kc, vc, n_even, n_odd):
    kce, kco = kc
    vce, vco = vc
    outs = []
    for par, kmat, n_par in ((0, kce, n_even), (1, kco, n_odd)):
        cend = lane * (2 * CMP_BLK) + (par * CMP_BLK + CMP_BLK - 1)
        valid = (rowq >= cend) & (lane < n_par)
        s = _dot_nt(qm, kmat) - slope * (qposf - cend.astype(F32))
        outs.append((valid, jnp.where(valid, s, NEG)))
    mx = jnp.maximum(jnp.max(outs[0][1], axis=-1, keepdims=True), jnp.max(outs[1][1], axis=-1, keepdims=True))
    ee = jnp.where(outs[0][0], jnp.exp(outs[0][1] - mx), 0.0)
    eo = jnp.where(outs[1][0], jnp.exp(outs[1][1] - mx), 0.0)
    den = jnp.maximum(jnp.sum(ee, axis=-1, keepdims=True) + jnp.sum(eo, axis=-1, keepdims=True), TINY)
    pe, po = ee / den, eo / den
    return _dot(pe.astype(BF16), vce) + _dot(po.astype(BF16), vco), pe + po


def _top_blocks(imp, rowq, lane, n_blocks, n_pick):
    jq = rowq >> 6
    forced = (lane == 0) | (lane == jq) | (lane == jq - 1)
    score = jnp.where((lane <= jq) & (lane < n_blocks), imp + jnp.where(forced, FORCE, 0.0), NEG)
    lanef = lane.astype(F32)
    sel = jnp.zeros(imp.shape, F32)
    picks = []
    for _ in range(n_pick):
        mx = jnp.max(score, axis=-1, keepdims=True)
        am = jnp.min(jnp.where(score == mx, lanef, float(LANES)), axis=-1, keepdims=True)
        hit = lanef == am
        picked = mx > 0.5 * NEG
        sel = jnp.where(hit, jnp.where(picked, 1.0, 0.0), sel)
        score = jnp.where(hit, REMOVED, score)
        picks.append(jnp.where(picked, am, -1.0))
    return sel, picks


def _nsa_prompt_body(sl_ref, q_ref, gn_ref, z_ref, nsa_ref, win_ref, cw_ref, a_ref,
                     kce, kco, vce, vco, ksb, vsb, kwb, vwb, *, tq, seq):
    i = pl.program_id(1)
    nc = seq // CMP_BLK
    n_even, n_odd = (nc + 1) // 2, nc // 2

    @pl.when(i == 0)
    def _():
        ksb[...] = nsa_ref[:, 256:384].astype(BF16)
        vsb[...] = nsa_ref[:, 384:512].astype(BF16)
        kwb[...] = win_ref[:, 0:128].astype(BF16)
        vwb[...] = win_ref[:, 128:256].astype(BF16)
        for ref in (kce, kco, vce, vco):
            ref[...] = jnp.zeros(ref.shape, F32)
        cw = cw_ref[...]
        rows8 = _iota((SUB, 256), 0)
        for t8 in range(-(-n_even // SUB)):
            te = jnp.zeros((SUB, 256), F32)
            to = jnp.zeros((SUB, 256), F32)
            for e in range(SUB):
                n0 = 2 * (t8 * SUB + e)
                if n0 < nc:
                    te = jnp.where(rows8 == e, _block_sum(nsa_ref, n0 * CMP_BLK, cw), te)
                if n0 + 1 < nc:
                    to = jnp.where(rows8 == e, _block_sum(nsa_ref, (n0 + 1) * CMP_BLK, cw), to)
            rs = slice(t8 * SUB, (t8 + 1) * SUB)
            kce[rs, :], vce[rs, :] = te[:, :128], te[:, 128:]
            kco[rs, :], vco[rs, :] = to[:, :128], to[:, 128:]

    lane = _iota((tq, LANES), 1)
    rowq = _iota((tq, LANES), 0) + i * tq
    qposf = rowq.astype(F32)
    rel = (_iota((tq, tq), 0) - _iota((tq, tq), 1)).astype(F32)
    rel4 = jnp.concatenate([rel] * NS_R, axis=0)
    kc = (kce[...].astype(BF16), kco[...].astype(BF16))
    vc = (vce[...].astype(BF16), vco[...].astype(BF16))
    gates = _sigmoid(gn_ref[...])
    outs = []
    for g in range(NS_KV):
        qms, o_cmp = [], []
        imp = jnp.zeros((tq, LANES), F32)
        for r in range(NS_R):
            qm = _half_mask(_scaled_q(q_ref[:, r * 128:(r + 1) * 128], NS_HD ** -0.5), lane, g == 1)
            o, p = _cmp_attend(qm, sl_ref[g * NS_R + r], qposf, rowq, lane, kc, vc, n_even, n_odd)
            qms.append(qm)
            o_cmp.append(o)
            imp = imp + p
        sel, _ = _top_blocks(imp, rowq, lane, -(-seq // SEL_BLK), SEL_N)
        sel_b = sel.astype(BF16)
        q4 = jnp.concatenate(qms, axis=0)
        slope4 = jnp.concatenate([jnp.full((tq, 1), sl_ref[g * NS_R + r], F32) for r in range(NS_R)], axis=0)

        def attend(kt, carry, k_ref, v_ref, allowed):
            off = pl.multiple_of(kt * tq, tq)
            k = k_ref[pl.ds(off, tq), :]
            v = v_ref[pl.ds(off, tq), :]
            d0 = ((i - kt) * tq).astype(F32)
            bias = jnp.where(allowed(kt, rel + d0), 0.0, NEG)
            s = _dot_nt(q4, k) - slope4 * (rel4 + d0) + jnp.concatenate([bias] * NS_R, axis=0)
            return _softmax_update(carry, s, lambda p: _dot(p, v))

        def sel_allowed(kt, dist):
            blk = (_iota((LANES, tq), 1) + kt * tq) >> 6
            expand = jnp.where(_iota((LANES, tq), 0) == blk, 1.0, 0.0).astype(BF16)
            return (_dot(sel_b, expand) > 0.5) & (dist >= 0)

        def win_allowed(kt, dist):
            return (dist >= 0) & (dist <= WINDOW)

        _, l_sel, a_sel = lax.fori_loop(
            0, i + 1, lambda kt, c: attend(kt, c, ksb, vsb, sel_allowed), _softmax_init(NS_R * tq))
        _, l_win, a_win = lax.fori_loop(
            jnp.maximum(i - WINDOW // tq, 0), i + 1, lambda kt, c: attend(kt, c, kwb, vwb, win_allowed),
            _softmax_init(NS_R * tq))
        o_sel, o_win = a_sel / l_sel, a_win / l_win
        og = []
        for r in range(NS_R):
            col = (g * NS_R + r) * 3
            rs = slice(r * tq, (r + 1) * tq)
            og.append(gates[:, col:col + 1] * o_cmp[r] + gates[:, col + 1:col + 2] * o_sel[rs]
                      + gates[:, col + 2:col + 3] * o_win[rs])
        outs.append(og)
    for r in range(NS_R):
        cs = slice(r * 128, (r + 1) * 128)
        o = jnp.where(lane < 64, outs[0][r], outs[1][r])
        a_ref[:, cs] = (o * _silu(z_ref[:, cs].astype(F32))).astype(BF16)


def _nsa_prompt(sl, ub, uf, ukv, cw, batch, seq):
    tq = min(128, seq)
    nq = seq // tq
    row = lambda b, i: b * nq + i
    return pl.pallas_call(
        functools.partial(_nsa_prompt_body, tq=tq, seq=seq),
        out_shape=jax.ShapeDtypeStruct((batch * seq, 512), BF16),
        grid=(batch, nq),
        in_specs=[_smem(),
                  pl.BlockSpec((tq, 512), lambda b, i: (row(b, i), UB_NSQ // 512)),
                  pl.BlockSpec((tq, 128), lambda b, i: (row(b, i), UF_NSG // 128)),
                  pl.BlockSpec((tq, 512), lambda b, i: (row(b, i), UB_NSZ // 512)),
                  pl.BlockSpec((seq, 512), lambda b, i: (b, KV_NSA // 512)),
                  pl.BlockSpec((seq, 256), lambda b, i: (b, KV_WIN // 256)),
                  pl.BlockSpec((CMP_BLK, 256), lambda b, i: (0, 0))],
        out_specs=pl.BlockSpec((tq, 512), lambda b, i: (row(b, i), 0)),
        scratch_shapes=[pltpu.VMEM((LANES, 128), F32)] * 4 + [pltpu.VMEM((seq, 128), BF16)] * 4,
        compiler_params=_params("arbitrary", "arbitrary"),
        name="nsa_prompt",
    )(sl, ub, uf, ub, ukv, ukv, cw)


def _nsa_sample_a_body(pt_ref, sl_ref, q_ref, gn_ref, buf_ref, wnew_ref, cw_ref, *rest, PP, past, n_tok):
    del pt_ref
    pages = rest[:PP]
    part_ref, idx_ref = rest[PP:PP + 2]
    kce, kco, vce, vco = rest[PP + 2:]
    c = pl.program_id(1)
    nc = past // CMP_BLK
    n_even, n_odd = (nc + 1) // 2, nc // 2

    @pl.when(c == 0)
    def _():
        for ref in (kce, kco, vce, vco):
            ref[...] = jnp.zeros(ref.shape, F32)

    cw = cw_ref[...]
    rows8 = _iota((SUB, 256), 0)
    for t8 in range(PP // 4):
        te = jnp.zeros((SUB, 256), F32)
        to = jnp.zeros((SUB, 256), F32)
        for pp in range(4):
            for nn in range(4):
                bs = _block_sum(pages[t8 * 4 + pp], nn * CMP_BLK, cw)
                if nn % 2 == 0:
                    te = jnp.where(rows8 == pp * 2 + nn // 2, bs, te)
                else:
                    to = jnp.where(rows8 == pp * 2 + nn // 2, bs, to)
        off = pl.multiple_of(c * (2 * PP) + t8 * SUB, SUB)
        kce[pl.ds(off, SUB), :], vce[pl.ds(off, SUB), :] = te[:, :128], te[:, 128:]
        kco[pl.ds(off, SUB), :], vco[pl.ds(off, SUB), :] = to[:, :128], to[:, 128:]

    @pl.when(c == pl.num_programs(1) - 1)
    def _():
        lane = _iota((SUB, LANES), 1)
        rowt = _iota((SUB, LANES), 0)
        rowq = rowt + past
        qposf = rowq.astype(F32)
        kc = (kce[...].astype(BF16), kco[...].astype(BF16))
        vc = (vce[...].astype(BF16), vco[...].astype(BF16))
        gates = _sigmoid(gn_ref[...])
        nbuf = buf_ref.shape[0]
        kw = buf_ref[:, 0:128].astype(BF16)
        vw = buf_ref[:, 128:256].astype(BF16)
        pad = jnp.zeros((LANES - SUB, LANES), F32)
        kwn = jnp.concatenate([wnew_ref[:, 0:128], pad], axis=0).astype(BF16)
        vwn = jnp.concatenate([wnew_ref[:, 128:256], pad], axis=0).astype(BF16)
        rows4 = NS_R * SUB
        tok4 = _iota((rows4, 1), 0) & (SUB - 1)
        head4 = _iota((rows4, 1), 0) >> 3
        outs = []
        for g in range(NS_KV):
            qms, o_cmp = [], []
            imp = jnp.zeros((SUB, LANES), F32)
            for r in range(NS_R):
                qm = _half_mask(_scaled_q(q_ref[:, r * 128:(r + 1) * 128], NS_HD ** -0.5), lane, g == 1)
                o, p = _cmp_attend(qm, sl_ref[g * NS_R + r], qposf, rowq, lane, kc, vc, n_even, n_odd)
                qms.append(qm)
                o_cmp.append(o)
                imp = imp + p
            _, picks = _top_blocks(imp, rowq, lane, past // SEL_BLK, SEL_N - 1)
            idx = jnp.full((SUB, LANES), -1.0, F32)
            for kk, pk in enumerate(picks):
                idx = jnp.where(lane == kk, pk, idx)
            idx_ref[g] = idx.astype(jnp.int32)
            q4 = jnp.concatenate(qms, axis=0)
            slope4 = jnp.zeros((rows4, 1), F32)
            for r in range(NS_R):
                slope4 = jnp.where(head4 == r, sl_ref[g * NS_R + r], slope4)
            dist_b = tok4 + (nbuf - _iota((rows4, nbuf), 1))
            s_b = jnp.where((dist_b >= 0) & (dist_b <= WINDOW),
                            _dot_nt(q4, kw) - slope4 * dist_b.astype(F32), NEG)
            jn = _iota((rows4, LANES), 1)
            s_n = jnp.where((jn <= tok4) & (jn < n_tok), _dot_nt(q4, kwn) - slope4 * (tok4 - jn).astype(F32), NEG)
            mx = jnp.maximum(jnp.max(s_b, axis=-1, keepdims=True), jnp.max(s_n, axis=-1, keepdims=True))
            p_b, p_n = jnp.exp(s_b - mx), jnp.exp(s_n - mx)
            den = jnp.sum(p_b, axis=-1, keepdims=True) + jnp.sum(p_n, axis=-1, keepdims=True)
            o_win = (_dot(p_b.astype(BF16), vw) + _dot(p_n.astype(BF16), vwn)) / den
            og = []
            for r in range(NS_R):
                col = (g * NS_R + r) * 3
                og.append(gates[:, col:col + 1] * o_cmp[r]
                          + gates[:, col + 2:col + 3] * o_win[r * SUB:(r + 1) * SUB])
            outs.append(og)
        for r in range(NS_R):
            part_ref[:, r * 128:(r + 1) * 128] = jnp.where(lane < 64, outs[0][r], outs[1][r])


def _nsa_sample_a(l, sl, pt_flat, ub, uf, ukv, buf, cw, cache, batch, n_pages, n_phys, n_tok):
    PP = min(16, n_pages)
    past = n_pages * PAGE_SIZE
    nbuf = buf.shape[2]

    def page_spec(p):
        return pl.BlockSpec((None, PAGE_SIZE, 256),
                            lambda b, c, pt: (l * n_phys + pt[b * n_pages + c * PP + p], 0, 0))

    grid_spec = pltpu.PrefetchScalarGridSpec(
        num_scalar_prefetch=1,
        grid=(batch, n_pages // PP),
        in_specs=[_smem(),
                  pl.BlockSpec((SUB, 512), lambda b, c, pt: (b, UB_NSQ // 512)),
                  pl.BlockSpec((SUB, 128), lambda b, c, pt: (b, UF_NSG // 128)),
                  pl.BlockSpec((None, None, nbuf, 256), lambda b, c, pt: (l, b, 0, 0)),
                  pl.BlockSpec((SUB, 256), lambda b, c, pt: (b, KV_WIN // 256)),
                  pl.BlockSpec((CMP_BLK, 256), lambda b, c, pt: (0, 0))] + [page_spec(p) for p in range(PP)],
        out_specs=(pl.BlockSpec((SUB, 512), lambda b, c, pt: (b, 0)),
                   pl.BlockSpec((None, NS_KV, SUB, LANES), lambda b, c, pt: (b, 0, 0, 0))),
        scratch_shapes=[pltpu.VMEM((LANES, 128), F32)] * 4)
    return pl.pallas_call(
        functools.partial(_nsa_sample_a_body, PP=PP, past=past, n_tok=n_tok),
        out_shape=(jax.ShapeDtypeStruct((batch * SUB, 512), F32),
                   jax.ShapeDtypeStruct((batch, NS_KV, SUB, LANES), jnp.int32)),
        grid_spec=grid_spec,
        compiler_params=_params("arbitrary", "arbitrary"),
        name="nsa_sample_a",
    )(pt_flat, sl, ub, uf, buf, ukv, cw, *([cache] * PP))


def _nsa_sample_b_body(pt_ref, idx_ref, sl_ref, q_ref, gn_ref, z_ref, new_ref, part_ref, *rest, NSEL, past, n_tok):
    del pt_ref
    blocks = rest[:NSEL]
    a_ref = rest[NSEL]
    acc_scr = rest[NSEL + 1]
    b, g, t = pl.program_id(0), pl.program_id(1), pl.program_id(2)

    @pl.when((g == 0) & (t == 0))
    def _():
        acc_scr[...] = part_ref[...]

    lane = _iota((SUB, LANES), 1)
    rows = _iota((SUB, LANES), 0)
    ghalf = (lane >> 6) == g
    rcol = _iota((SUB, 1), 0)
    gates = _sigmoid(gn_ref[...])
    q = jnp.zeros((SUB, LANES), F32)
    slope = jnp.zeros((SUB, 1), F32)
    gate = jnp.zeros((SUB, 1), F32)
    for r in range(NS_R):
        qp = q_ref[:, r * 128:(r + 1) * 128].astype(F32)
        q = jnp.where(rows == r, jnp.sum(jnp.where(rows == t, qp, 0.0), axis=0, keepdims=True), q)
        slope = jnp.where(rcol == r, sl_ref[g * NS_R + r], slope)
        gv = jnp.where((rows == t) & (lane == (g * NS_R + r) * 3 + 1), gates, 0.0)
        gv = jnp.sum(jnp.sum(gv, axis=1, keepdims=True), axis=0, keepdims=True)
        gate = jnp.where(rcol == r, gv, gate)
    qb = jnp.where(ghalf, q * NS_HD ** -0.5, 0.0).astype(BF16)
    qpos = past + t
    base = ((b * NS_KV + g) * n_tok + t) * SEL_N
    pad = jnp.zeros((SEL_BLK - SUB, LANES), F32)
    knew = jnp.concatenate([new_ref[:, 0:128], pad], axis=0)
    vnew = jnp.concatenate([new_ref[:, 128:256], pad], axis=0)
    low = lane < SEL_BLK
    s_all, v_all = [], []
    for kp in range((NSEL + 1) // 2):
        b0 = idx_ref[base + 2 * kp]
        k0, v0 = blocks[2 * kp][:, 0:128], blocks[2 * kp][:, 128:256]
        if 2 * kp + 1 < NSEL:
            b1 = idx_ref[base + 2 * kp + 1]
            k1, v1 = blocks[2 * kp + 1][:, 0:128], blocks[2 * kp + 1][:, 128:256]
            blk = jnp.where(low, b0, b1)
            kpos = blk * SEL_BLK + (lane & (SEL_BLK - 1))
            ok = blk >= 0
        else:
            k1, v1 = knew, vnew
            j = lane - SEL_BLK
            kpos = jnp.where(low, b0 * SEL_BLK + lane, past + j)
            ok = jnp.where(low, b0, jnp.where((j <= t) & (j < n_tok), 0, -1)) >= 0
        kk = jnp.concatenate([k0, k1], axis=0).astype(BF16)
        s = _dot_nt(qb, kk) - slope * (qpos - kpos).astype(F32)
        s_all.append(jnp.where(ok, s, NEG))
        v_all.append(jnp.concatenate([v0, v1], axis=0).astype(BF16))
    mx = s_all[0].max(axis=-1, keepdims=True)
    for s in s_all[1:]:
        mx = jnp.maximum(mx, s.max(axis=-1, keepdims=True))
    den = jnp.zeros((SUB, 1), F32)
    o = jnp.zeros((SUB, LANES), F32)
    for s, v in zip(s_all, v_all):
        p = jnp.exp(s - mx)
        den = den + jnp.sum(p, axis=-1, keepdims=True)
        o = o + _dot(p.astype(BF16), v)
    o = gate * o / den
    for r in range(NS_R):
        cs = slice(r * 128, (r + 1) * 128)
        acc_scr[:, cs] = acc_scr[:, cs] + jnp.where((rows == t) & ghalf, o[r:r + 1, :], 0.0)

    @pl.when((g == NS_KV - 1) & (t == n_tok - 1))
    def _():
        a_ref[...] = (acc_scr[...] * _silu(z_ref[...].astype(F32))).astype(BF16)


def _nsa_sample_b(l, sl, pt_flat, idx_flat, ub, uf, ukv, part, cache_half, batch, n_pages, n_phys, n_tok):
    NSEL = SEL_N - 1
    past = n_pages * PAGE_SIZE
    halves = PAGE_SIZE // SEL_BLK

    def blk_spec(k):
        def index(b, g, t, pt, idx):
            blk = jnp.maximum(idx[((b * NS_KV + g) * n_tok + t) * SEL_N + k], 0)
            page = pt[b * n_pages + blk // halves]
            return ((l * n_phys + page) * halves + blk % halves, 0, 1)
        return pl.BlockSpec((None, SEL_BLK, 256), index)

    grid_spec = pltpu.PrefetchScalarGridSpec(
        num_scalar_prefetch=2,
        grid=(batch, NS_KV, n_tok),
        in_specs=[_smem(),
                  pl.BlockSpec((SUB, 512), lambda b, g, t, pt, idx: (b, UB_NSQ // 512)),
                  pl.BlockSpec((SUB, 128), lambda b, g, t, pt, idx: (b, UF_NSG // 128)),
                  pl.BlockSpec((SUB, 512), lambda b, g, t, pt, idx: (b, UB_NSZ // 512)),
                  pl.BlockSpec((SUB, 256), lambda b, g, t, pt, idx: (b, (KV_NSA + 256) // 256)),
                  pl.BlockSpec((SUB, 512), lambda b, g, t, pt, idx: (b, 0))] + [blk_spec(k) for k in range(NSEL)],
        out_specs=pl.BlockSpec((SUB, 512), lambda b, g, t, pt, idx: (b, 0)),
        scratch_shapes=[pltpu.VMEM((SUB, 512), F32)])
    return pl.pallas_call(
        functools.partial(_nsa_sample_b_body, NSEL=NSEL, past=past, n_tok=n_tok),
        out_shape=jax.ShapeDtypeStruct((batch * SUB, 512), BF16),
        grid_spec=grid_spec,
        compiler_params=_params("arbitrary", "arbitrary", "arbitrary"),
        name="nsa_sample_b",
    )(pt_flat, idx_flat, sl, ub, uf, ub, ukv, part, *([cache_half] * NSEL))


def _final_body(x_ref, gate_ref, gp_ref, ahg_ref, adf_ref, ans_ref, mg_ref, whg_ref, wdf_ref, wns_ref, wo_ref, o_ref):
    d = x_ref.shape[-1]
    merged = (_sigmoid(mg_ref[:, 0:d].astype(F32)) * _dot(ahg_ref[...], whg_ref[...])
              + _sigmoid(mg_ref[:, d:2 * d].astype(F32)) * _dot(adf_ref[...], wdf_ref[...])
              + _sigmoid(mg_ref[:, 2 * d:3 * d].astype(F32)) * _dot(ans_ref[...], wns_ref[...]))
    out = _dot(merged.astype(BF16), wo_ref[...])
    ms = jnp.mean(out * out, axis=-1, keepdims=True)
    o_ref[...] = x_ref[...] + gate_ref[...] * (out * lax.rsqrt(ms + EPS) * gp_ref[...])


def _final(x2, cond, l, g_post, a_hg, a_df, a_ns, ub, w_hg, w_df, w_ns, w_o, tm, rows_per_batch, name):
    rows, d = x2.shape
    wspec = lambda k: pl.BlockSpec((None, k, d), lambda i: (l, 0, 0))
    return pl.pallas_call(
        _final_body,
        out_shape=jax.ShapeDtypeStruct((rows, d), F32),
        grid=(rows // tm,),
        in_specs=[pl.BlockSpec((tm, d), lambda i: (i, 0)),
                  _cond_spec(cond, l, 2, tm, rows_per_batch),
                  pl.BlockSpec((None, 1, d), lambda i: (l, 0, 0)),
                  pl.BlockSpec((tm, 512), lambda i: (i, 0)),
                  pl.BlockSpec((tm, 512), lambda i: (i, 0)),
                  pl.BlockSpec((tm, 512), lambda i: (i, 0)),
                  pl.BlockSpec((tm, 3 * d), lambda i: (i, UB_MG // (3 * d))),
                  wspec(512), wspec(512), wspec(512), wspec(d)],
        out_specs=pl.BlockSpec((tm, d), lambda i: (i, 0)),
        compiler_params=_params("arbitrary"),
        name=name,
    )(x2, cond, g_post, a_hg, a_df, a_ns, ub, w_hg, w_df, w_ns, w_o)


def _alibi(n):
    return [2.0 ** (-8.0 * (h + 1) / n) for h in range(n)]


def kernel(x_prompt, x_sample, cache_diff, cache_nsa, cache_nsa_win, state_hgrn, page_table, c_prompt, c_sample,
           w_cond, b_cond, g_pre, g_post, w_in, hg_lb, hg_norm, df_lam, df_norm, ns_cmp,
           w_hg_out, w_df_out, w_ns_out, w_out):
    depth, d, _ = w_in.shape
    bp, seq, _ = x_prompt.shape
    bs, n_tok, _ = x_sample.shape
    n_pages = page_table.shape[1]
    n_phys = cache_diff.shape[1]
    past = n_pages * PAGE_SIZE
    assert d == 1024 and seq % 128 == 0 and n_tok <= SUB and n_pages % 4 == 0
    assert cache_nsa_win.shape[2] == WINDOW and past >= WINDOW and past // SEL_BLK <= LANES

    kv_cols, uf_cols, ub_cols, pair = _column_layout()
    w_kv = jnp.take(w_in, kv_cols, axis=2).astype(BF16)
    w_uf = jnp.where(uf_cols >= 0, jnp.take(w_in, np.maximum(uf_cols, 0), axis=2), 0.0).astype(BF16)
    w_ub = jnp.take(w_in, ub_cols, axis=2).astype(BF16)
    w_hg = w_hg_out.astype(BF16)
    w_df = w_df_out.astype(BF16)
    w_ns = jnp.take(w_ns_out, pair, axis=1).astype(BF16)
    w_o = w_out.astype(BF16)
    g_pre3 = g_pre.reshape(depth, 1, d)
    g_post3 = g_post.reshape(depth, 1, d)
    lb_w = jax.nn.softmax(hg_lb.astype(F32), axis=0)
    lower = jnp.cumsum(lb_w, axis=0) - lb_w[0]
    cw_all = jax.nn.softmax(ns_cmp.astype(F32), axis=-1)
    cw_tile = jnp.repeat(jnp.swapaxes(cw_all, 1, 2), 128, axis=2)
    lv = df_lam.astype(F32)
    lam_init = jnp.asarray([0.8 - 0.6 * math.exp(-0.3 * l) for l in range(depth)], F32)
    lam = jnp.exp(jnp.sum(lv[:, 0] * lv[:, 1], axis=-1)) - jnp.exp(jnp.sum(lv[:, 2] * lv[:, 3], axis=-1)) + lam_init
    df_sc = jnp.concatenate([lam[:, None], 1.0 - lam_init[:, None],
                             jnp.broadcast_to(jnp.asarray(_alibi(DF_HEADS), F32), (depth, DF_HEADS))], axis=1)
    ns_sl = jnp.asarray(_alibi(NS_HEADS), F32)

    bc = bp + bs
    bc_pad = -(-bc // SUB) * SUB
    c_all = jnp.concatenate([c_prompt, c_sample, jnp.zeros((bc_pad - bc, d), F32)], axis=0)
    cond = _cond_all(c_all, w_cond, b_cond)
    cond_p = cond[:, :bp].reshape(depth, bp, 1, 3 * d)
    cond_s = jnp.repeat(cond[:, bp:bc], SUB, axis=1)

    pt_flat = page_table.reshape(-1).astype(jnp.int32)
    cache_diff2 = cache_diff.reshape(depth * n_phys, PAGE_SIZE, 512)
    cache_nsa2 = cache_nsa.reshape(depth * n_phys, PAGE_SIZE, 512)
    cache_nsa_half = cache_nsa.reshape(depth * n_phys * (PAGE_SIZE // SEL_BLK), SEL_BLK, 512)
    win_buf = cache_nsa_win.reshape(depth, bs, WINDOW, 256)

    xp = x_prompt.reshape(bp * seq, d)
    xs = jnp.pad(x_sample, ((0, 0), (0, SUB - n_tok), (0, 0))).reshape(bs * SUB, d)
    zeros_state = jnp.zeros((bp, HG_HEADS, HG_DK, HG_DV), F32)
    tm_p = min(1024, seq)
    tm_f = min(512, seq)
    rows_s = bs * SUB

    outs = {k: [] for k in ('dp', 'ds', 'np', 'ns', 'wp', 'ws', 'hp', 'hs')}
    for l in range(depth):
        lb = lower[l].reshape(1, 512)
        nw_hg = hg_norm[l].reshape(1, HG_DV)
        nw_df = df_norm[l].reshape(1, DF_VD)

        ukv = _inproj(xp, cond_p, l, g_pre3, w_kv, KV_W, F32, tm_p, seq, "inproj_kv_p")
        uf = _inproj(xp, cond_p, l, g_pre3, w_uf, UF_W, F32, tm_p, seq, "inproj_f_p")
        ub = _inproj(xp, cond_p, l, g_pre3, w_ub, 512, BF16, tm_p, seq, "inproj_b_p")
        a_hg, s_p = _hgrn_prompt(ub, uf, lb, nw_hg, zeros_state, bp, seq)
        a_df = _diff_prompt(df_sc[l], ub, ukv, nw_df, bp, seq)
        a_ns = _nsa_prompt(ns_sl, ub, uf, ukv, cw_tile[l], bp, seq)
        xp = _final(xp, cond_p, l, g_post3, a_hg, a_df, a_ns, ub, w_hg, w_df, w_ns, w_o, tm_f, seq, "final_p")
        ukv3 = ukv.reshape(bp, seq, KV_W)
        outs['dp'].append(ukv3[:, :, KV_DIFF:KV_NSA].reshape(bp, seq, DF_KV, 256))
        outs['np'].append(ukv3[:, :, KV_NSA:KV_WIN].reshape(bp, seq, 4, NS_KV, NS_HD))
        outs['wp'].append(ukv3[:, seq - min(WINDOW, seq):, KV_WIN:].reshape(bp, min(WINDOW, seq), 2, NS_KV, NS_HD))
        outs['hp'].append(s_p)

        ukv = _inproj(xs, cond_s, l, g_pre3, w_kv, KV_W, F32, rows_s, SUB, "inproj_kv_s")
        uf = _inproj(xs, cond_s, l, g_pre3, w_uf, UF_W, F32, rows_s, SUB, "inproj_f_s")
        ub = _inproj(xs, cond_s, l, g_pre3, w_ub, 512, BF16, rows_s, SUB, "inproj_b_s")
        a_hg, s_s = _hgrn_sample(ub, uf, lb, nw_hg, state_hgrn[l], bs, n_tok)
        a_df = _diff_sample(l, df_sc[l], pt_flat, ub, ukv, nw_df, cache_diff2, bs, n_pages, n_phys, n_tok)
        part, idx = _nsa_sample_a(l, ns_sl, pt_flat, ub, uf, ukv, win_buf, cw_tile[l], cache_nsa2,
                                  bs, n_pages, n_phys, n_tok)
        idx_flat = idx[:, :, :n_tok, :SEL_N].reshape(-1)
        a_ns = _nsa_sample_b(l, ns_sl, pt_flat, idx_flat, ub, uf, ukv, part, cache_nsa_half,
                             bs, n_pages, n_phys, n_tok)
        xs = _final(xs, cond_s, l, g_post3, a_hg, a_df, a_ns, ub, w_hg, w_df, w_ns, w_o, rows_s, SUB, "final_s")
        ukv3 = ukv.reshape(bs, SUB, KV_W)[:, :n_tok]
        outs['ds'].append(ukv3[:, :, KV_DIFF:KV_NSA].reshape(bs, n_tok, DF_KV, 256))
        outs['ns'].append(ukv3[:, :, KV_NSA:KV_WIN].reshape(bs, n_tok, 4, NS_KV, NS_HD))
        new_w = ukv3[:, :, KV_WIN:].reshape(bs, n_tok, 2, NS_KV, NS_HD)
        outs['ws'].append(jnp.concatenate([cache_nsa_win[l][:, n_tok:], new_w], axis=1))
        outs['hs'].append(s_s)

    st = {k: jnp.stack(v) for k, v in outs.items()}
    y_p = xp.reshape(bp, seq, d)
    y_s = xs.reshape(bs, SUB, d)[:, :n_tok]
    return (y_p, y_s, st['dp'], st['ds'], st['np'], st['ns'], st['wp'], st['ws'], st['hp'], st['hs'])
```

```python
import functools
import math

import numpy as np
import jax
import jax.numpy as jnp
from jax import lax
from jax.experimental import pallas as pl
from jax.experimental.pallas import tpu as pltpu

F32 = jnp.float32
BF16 = jnp.bfloat16

PAGE_SIZE = 128
HG_HEADS, HG_DK, HG_DV = 4, 128, 128
HG_CHUNK = 64
DF_HEADS, DF_KV, DF_HD = 4, 2, 64
DF_VD = 2 * DF_HD
NS_HEADS, NS_KV, NS_HD = 8, 2, 64
NS_R = NS_HEADS // NS_KV
CMP_BLK, SEL_BLK, SEL_N, WINDOW = 32, 64, 8, 512
EPS, NEG, TINY, FORCE = 1e-6, -1e30, 1e-30, 1e4
REMOVED = -3e38
EXP_CLAMP = 80.0
LOG2E = math.log2(math.e)
LANES = 128
SUB = 8
CMP_PAGES = 32
VMEM_LIMIT = 48 * 1024 * 1024

IN_SPLITS = (
    ('hg_q', 512), ('hg_f', 512), ('hg_i', 512), ('hg_z', 512),
    ('df_q', 512), ('df_k', 256), ('df_v', 256), ('df_z', 512),
    ('ns_q', 512), ('ns_kvc', 256), ('ns_kvs', 256), ('ns_kvw', 256), ('ns_g', 24), ('ns_z', 512),
    ('merge', 3072),
)

KV_DIFF, KV_NSA, KV_WIN, KV_W = 0, 512, 1024, 1280
UF_HGF, UF_NSG, UF_W = 0, 512, 640
UB_MG, UB_HGQ, UB_HGI, UB_DFQ, UB_NSQ, UB_HGZ, UB_DFZ, UB_NSZ, UB_W = (
    0, 3072, 3584, 4096, 4608, 5120, 5632, 6144, 6656)


def _column_layout():
    off, o = {}, 0
    for n, w in IN_SPLITS:
        off[n] = o
        o += w
    rng = lambda n, a, b: list(range(off[n] + a, off[n] + b))
    kv = []
    for g in range(DF_KV):
        kv += rng('df_k', g * 128, (g + 1) * 128) + rng('df_v', g * 128, (g + 1) * 128)
    kv += rng('ns_kvc', 0, 256) + rng('ns_kvs', 0, 256) + rng('ns_kvw', 0, 256)
    uf = rng('hg_f', 0, 512) + rng('ns_g', 0, 24) + [-1] * 104
    pair = [g * 256 + r * 64 + d for r in range(NS_R) for g in range(NS_KV) for d in range(NS_HD)]
    ub = (rng('merge', 0, 3072) + rng('hg_q', 0, 512) + rng('hg_i', 0, 512) + rng('df_q', 0, 512)
          + [off['ns_q'] + p for p in pair] + rng('hg_z', 0, 512) + rng('df_z', 0, 512)
          + [off['ns_z'] + p for p in pair])
    assert len(kv) == KV_W and len(uf) == UF_W and len(ub) == UB_W
    return np.asarray(kv), np.asarray(uf), np.asarray(ub), np.asarray(pair)


def _dot(a, b):
    return jnp.dot(a, b, preferred_element_type=F32)


def _dot_nt(a, b):
    return lax.dot_general(a, b, (((1,), (1,)), ((), ())), preferred_element_type=F32)


def _sigmoid(x):
    return 1.0 / (1.0 + jnp.exp(-x))


def _silu(x):
    return x * _sigmoid(x)


def _iota(shape, dim):
    return lax.broadcasted_iota(jnp.int32, shape, dim)


def _params(*sem):
    return pltpu.CompilerParams(dimension_semantics=sem, vmem_limit_bytes=VMEM_LIMIT)


def _smem():
    return pl.BlockSpec(memory_space=pltpu.SMEM)


def _cond_body(c_ref, w_ref, b_ref, o_ref):
    a = _silu(c_ref[...])
    a_hi = a.astype(BF16)
    a_lo = (a - a_hi.astype(F32)).astype(BF16)
    w = w_ref[...]
    w_hi = w.astype(BF16)
    w_lo = (w - w_hi.astype(F32)).astype(BF16)
    o_ref[...] = _dot(a_hi, w_hi) + (_dot(a_hi, w_lo) + _dot(a_lo, w_hi)) + b_ref[...]


def _cond_all(c_all, w_cond, b_cond):
    depth, d, _ = w_cond.shape
    bc = c_all.shape[0]
    return pl.pallas_call(
        _cond_body,
        out_shape=jax.ShapeDtypeStruct((depth, bc, 3 * d), F32),
        grid=(depth, 3),
        in_specs=[pl.BlockSpec((bc, d), lambda l, k: (0, 0)),
                  pl.BlockSpec((None, d, d), lambda l, k: (l, 0, k)),
                  pl.BlockSpec((None, 1, d), lambda l, k: (l, 0, k))],
        out_specs=pl.BlockSpec((None, bc, d), lambda l, k: (l, 0, k)),
        compiler_params=_params("arbitrary", "arbitrary"),
        name="cond",
    )(c_all, w_cond, b_cond.reshape(depth, 1, 3 * d))


def _inproj_body(x_ref, sh_ref, sc_ref, gp_ref, w_ref, o_ref, h_scr):
    @pl.when(pl.program_id(1) == 0)
    def _():
        x = x_ref[...]
        ms = jnp.mean(x * x, axis=-1, keepdims=True)
        y = x * lax.rsqrt(ms + EPS) * gp_ref[...]
        h_scr[...] = (y * (1.0 + sc_ref[...]) + sh_ref[...]).astype(BF16)

    o_ref[...] = _dot(h_scr[...], w_ref[...]).astype(o_ref.dtype)


def _cond_spec(cond, l, k, tm, rows_per_batch):
    d = cond.shape[-1] // 3
    if cond.ndim == 4:
        tiles = rows_per_batch // tm
        return pl.BlockSpec((None, None, 1, d), lambda i, *_: (l, i // tiles, 0, k))
    return pl.BlockSpec((None, tm, d), lambda i, *_: (l, i, k))


def _inproj(x2, cond, l, g_pre, w, tn, out_dtype, tm, rows_per_batch, name):
    rows, d = x2.shape
    n = w.shape[-1]
    return pl.pallas_call(
        _inproj_body,
        out_shape=jax.ShapeDtypeStruct((rows, n), out_dtype),
        grid=(rows // tm, n // tn),
        in_specs=[pl.BlockSpec((tm, d), lambda i, j: (i, 0)),
                  _cond_spec(cond, l, 0, tm, rows_per_batch),
                  _cond_spec(cond, l, 1, tm, rows_per_batch),
                  pl.BlockSpec((None, 1, d), lambda i, j: (l, 0, 0)),
                  pl.BlockSpec((None, d, tn), lambda i, j: (l, 0, j))],
        out_specs=pl.BlockSpec((tm, tn), lambda i, j: (i, j)),
        scratch_shapes=[pltpu.VMEM((tm, d), BF16)],
        compiler_params=_params("arbitrary", "arbitrary"),
        name=name,
    )(x2, cond, cond, g_pre, w)


def _hgrn_gates(fr, lbh):
    e = jnp.exp(-jnp.abs(fr))
    inv = 1.0 / (1.0 + e)
    pos = fr >= 0
    sg = jnp.where(pos, inv, e * inv)
    sgn = jnp.where(pos, e * inv, inv)
    fg = lbh + (1.0 - lbh) * sg
    return fg, jnp.log(jnp.maximum(fg, TINY)), (1.0 - lbh) * sgn


def _hgrn_out(o, nw, z_ref, rs, hs, a_ref):
    ms = jnp.mean(o * o, axis=-1, keepdims=True)
    on = o * lax.rsqrt(ms + EPS) * nw
    a_ref[rs, hs] = (on * _silu(z_ref[rs, hs].astype(F32))).astype(BF16)


def _hgrn_chunk_body(q_ref, v_ref, f_ref, z_ref, lb_ref, nw_ref, s0_ref, a_ref, s_out_ref, s_scr, *, C, n_chunks):
    t = pl.program_id(1)

    @pl.when(t == 0)
    def _():
        s_scr[...] = s0_ref[...]

    tri = _iota((C, C), 0) >= _iota((C, C), 1)
    tri_b = jnp.where(tri, 1.0, 0.0).astype(BF16)
    mid = C // 2
    nw = nw_ref[...]
    for h in range(HG_HEADS):
        hs = slice(h * HG_DK, (h + 1) * HG_DK)
        lbh = lb_ref[:, hs]
        S = s_scr[h]
        for c in range(n_chunks):
            rs = slice(c * C, (c + 1) * C)
            _, g, k = _hgrn_gates(f_ref[rs, hs], lbh)
            q = _silu(q_ref[rs, hs].astype(F32))
            v = v_ref[rs, hs]
            g1 = g.astype(BF16)
            r1 = g - g1.astype(F32)
            g2 = r1.astype(BF16)
            g3 = (r1 - g2.astype(F32)).astype(BF16)
            G = _dot(tri_b, g1) + _dot(tri_b, g2) + _dot(tri_b, g3)
            g_mid = G[mid - 1:mid, :]
            g_last = G[C - 1:C, :]
            qe = q * jnp.exp(jnp.minimum(G - g_mid, EXP_CLAMP))
            ke = k * jnp.exp(jnp.minimum(g_mid - G, EXP_CLAMP))
            att = jnp.where(tri, _dot_nt(qe.astype(BF16), ke.astype(BF16)), 0.0)
            o = _dot((q * jnp.exp(G)).astype(BF16), S.astype(BF16)) + _dot(att.astype(BF16), v)
            _hgrn_out(o, nw, z_ref, rs, hs, a_ref)
            kd = k * jnp.exp(g_last - G)
            tile = jnp.concatenate(
                [kd, jnp.broadcast_to(jnp.exp(g_last), (SUB, HG_DK)),
                 jnp.zeros((LANES - C - SUB, HG_DK), F32)], axis=0)
            tile_t = tile.T
            S = tile_t[:, C:C + 1] * S + _dot(tile_t[:, :C].astype(BF16), v)
        s_scr[h] = S

    @pl.when(t == pl.num_programs(1) - 1)
    def _():
        s_out_ref[...] = s_scr[...]


def _hgrn_prompt(ub, uf, lb, nw, s0, batch, seq):
    ts = min(256, seq)
    C = math.gcd(ts, HG_CHUNK)
    nt = seq // ts
    row = lambda b, t: b * nt + t
    return pl.pallas_call(
        functools.partial(_hgrn_chunk_body, C=C, n_chunks=ts // C),
        out_shape=(jax.ShapeDtypeStruct((batch * seq, 512), BF16),
                   jax.ShapeDtypeStruct((batch, HG_HEADS, HG_DK, HG_DV), F32)),
        grid=(batch, nt),
        in_specs=[pl.BlockSpec((ts, 512), lambda b, t: (row(b, t), UB_HGQ // 512)),
                  pl.BlockSpec((ts, 512), lambda b, t: (row(b, t), UB_HGI // 512)),
                  pl.BlockSpec((ts, 512), lambda b, t: (row(b, t), UF_HGF // 512)),
                  pl.BlockSpec((ts, 512), lambda b, t: (row(b, t), UB_HGZ // 512)),
                  pl.BlockSpec((1, 512), lambda b, t: (0, 0)),
                  pl.BlockSpec((1, HG_DV), lambda b, t: (0, 0)),
                  pl.BlockSpec((None, HG_HEADS, HG_DK, HG_DV), lambda b, t: (b, 0, 0, 0))],
        out_specs=(pl.BlockSpec((ts, 512), lambda b, t: (row(b, t), 0)),
                   pl.BlockSpec((None, HG_HEADS, HG_DK, HG_DV), lambda b, t: (b, 0, 0, 0))),
        scratch_shapes=[pltpu.VMEM((HG_HEADS, HG_DK, HG_DV), F32)],
        compiler_params=_params("arbitrary", "arbitrary"),
        name="hgrn_prompt",
    )(ub, ub, uf, ub, lb, nw, s0)


def _hgrn_step_body(q_ref, v_ref, f_ref, z_ref, lb_ref, nw_ref, s0_ref, a_ref, s_out_ref, *, n_tok):
    rows = _iota((SUB, HG_DV), 0)
    nw = nw_ref[...]
    rs = slice(0, SUB)
    for h in range(HG_HEADS):
        hs = slice(h * HG_DK, (h + 1) * HG_DK)
        fg, _, k = _hgrn_gates(f_ref[:, hs], lb_ref[:, hs])
        q = _silu(q_ref[:, hs].astype(F32))
        v = v_ref[:, hs].astype(F32)
        tile = jnp.concatenate([fg, k, q, jnp.zeros((LANES - 3 * SUB, HG_DK), F32)], axis=0)
        tile_t = tile.T
        S = s0_ref[h]
        o = jnp.zeros((SUB, HG_DV), F32)
        for t in range(n_tok):
            S = S * tile_t[:, t:t + 1] + tile_t[:, SUB + t:SUB + t + 1] * v[t:t + 1, :]
            o_t = jnp.sum(S * tile_t[:, 2 * SUB + t:2 * SUB + t + 1], axis=0, keepdims=True)
            o = jnp.where(rows == t, o_t, o)
        s_out_ref[h] = S
        _hgrn_out(o, nw, z_ref, rs, hs, a_ref)


def _hgrn_sample(ub, uf, lb, nw, s0, batch, n_tok):
    return pl.pallas_call(
        functools.partial(_hgrn_step_body, n_tok=n_tok),
        out_shape=(jax.ShapeDtypeStruct((batch * SUB, 512), BF16),
                   jax.ShapeDtypeStruct((batch, HG_HEADS, HG_DK, HG_DV), F32)),
        grid=(batch,),
        in_specs=[pl.BlockSpec((SUB, 512), lambda b: (b, UB_HGQ // 512)),
                  pl.BlockSpec((SUB, 512), lambda b: (b, UB_HGI // 512)),
                  pl.BlockSpec((SUB, 512), lambda b: (b, UF_HGF // 512)),
                  pl.BlockSpec((SUB, 512), lambda b: (b, UB_HGZ // 512)),
                  pl.BlockSpec((1, 512), lambda b: (0, 0)),
                  pl.BlockSpec((1, HG_DV), lambda b: (0, 0)),
                  pl.BlockSpec((None, HG_HEADS, HG_DK, HG_DV), lambda b: (b, 0, 0, 0))],
        out_specs=(pl.BlockSpec((SUB, 512), lambda b: (b, 0)),
                   pl.BlockSpec((None, HG_HEADS, HG_DK, HG_DV), lambda b: (b, 0, 0, 0))),
        compiler_params=_params("arbitrary"),
        name="hgrn_sample",
    )(ub, ub, uf, ub, lb, nw, s0)


def _softmax_update(carry, s, v_fn):
    m, l, acc = carry
    m_new = jnp.maximum(m, jnp.max(s, axis=-1, keepdims=True))
    alpha = jnp.exp(m - m_new)
    p = jnp.exp(s - m_new)
    return m_new, alpha * l + jnp.sum(p, axis=-1, keepdims=True), alpha * acc + v_fn(p.astype(BF16))


def _softmax_init(rows, width=LANES):
    return (jnp.full((rows, 1), NEG, F32), jnp.zeros((rows, 1), F32), jnp.zeros((rows, width), F32))


def _half_mask(x, lane, upper):
    keep = (lane >= 64) if upper else (lane < 64)
    return jnp.where(keep, x, jnp.zeros_like(x))


def _scaled_q(q, scale):
    return (q.astype(F32) * scale).astype(BF16)


def _diff_finish(o, lam, oml, nw, z, r_rows):
    od = o[:r_rows] - lam * o[r_rows:]
    ms = jnp.mean(od * od, axis=-1, keepdims=True)
    return (od * lax.rsqrt(ms + EPS) * nw * oml * _silu(z.astype(F32))).astype(BF16)


def _tflash_init(n):
    return (jnp.full((1, n), NEG, F32), jnp.zeros((1, n), F32), jnp.zeros((LANES, n), F32))


def _tflash_update(carry, s, c_row, vt):
    m, l, acc = carry
    m_new = jnp.maximum(m, jnp.max(s, axis=0, keepdims=True) + c_row)
    p = jnp.exp2(s - (m_new - c_row))
    alpha = jnp.exp2(m - m_new)
    return m_new, alpha * l + jnp.sum(p, axis=0, keepdims=True), alpha * acc + _dot(vt, p.astype(BF16))


def _fill_transposed(dst3, src_ref, cols, seq, tk):
    for t in range(seq // LANES):
        r0 = t * LANES
        dst3[r0 // tk, :, r0 % tk:r0 % tk + LANES] = src_ref[r0:r0 + LANES, cols].T.astype(BF16)


def _untranspose(x_t, tq):
    return jnp.concatenate([x_t[:, c:c + LANES].T for c in range(0, tq, LANES)], axis=0)


def _rel_t(tk, tq):
    return (_iota((tk, tq), 1) - _iota((tk, tq), 0)).astype(F32)


def _diff_prompt_body(sc_ref, q_ref, kv_ref, z_ref, nw_ref, a_ref, kb, vt, *, tq, tk, seq):
    g = pl.program_id(1)
    i = pl.program_id(2)
    R = DF_HEADS // DF_KV

    @pl.when(i == 0)
    def _():
        kb[...] = kv_ref[:, 0:128].astype(BF16)
        _fill_transposed(vt, kv_ref, slice(128, 256), seq, tk)

    lam, oml = sc_ref[0], sc_ref[1]
    lane = _iota((tq, LANES), 1)
    rel = _rel_t(tk, tq)
    qs, nrel, srow = [], [], []
    for r in range(R):
        slope = sc_ref[2 + g * R + r] * LOG2E
        q = _scaled_q(q_ref[:, r * 128:(r + 1) * 128], DF_HD ** -0.5 * LOG2E)
        qs += [_half_mask(q, lane, False), _half_mask(q, lane, True)]
        nrel += [-slope * rel] * 2
        srow += [jnp.full((1, tq), slope, F32)] * 2
    q4 = jnp.concatenate(qs, axis=0)
    nrel4 = jnp.concatenate(nrel, axis=1)
    srow4 = jnp.concatenate(srow, axis=1)

    def tile(kt, carry, masked):
        off = pl.multiple_of(kt * tk, tk)
        d0 = (i * tq - kt * tk).astype(F32)
        s = _dot_nt(kb[pl.ds(off, tk), :], q4) + nrel4
        if masked:
            s = s + jnp.concatenate([jnp.where(rel + d0 >= 0, 0.0, NEG)] * (2 * R), axis=1)
        return _tflash_update(carry, s, -d0 * srow4, vt[kt])

    n_full = (i * tq + 1) // tk
    carry = lax.fori_loop(0, n_full, lambda kt, c: tile(kt, c, False), _tflash_init(2 * R * tq))
    _, l, acc = lax.fori_loop(n_full, ((i + 1) * tq + tk - 1) // tk, lambda kt, c: tile(kt, c, True), carry)
    o_t = acc / l
    for r in range(R):
        od = _untranspose(o_t[:, 2 * r * tq:(2 * r + 1) * tq] - lam * o_t[:, (2 * r + 1) * tq:(2 * r + 2) * tq], tq)
        ms = jnp.mean(od * od, axis=-1, keepdims=True)
        cs = slice(r * 128, (r + 1) * 128)
        a_ref[:, cs] = (od * lax.rsqrt(ms + EPS) * nw_ref[...] * oml * _silu(z_ref[:, cs].astype(F32))).astype(BF16)


def _diff_prompt(sc, ub, ukv, nw, batch, seq):
    tq = min(256, seq)
    tk = min(256, seq)
    nq = seq // tq
    row = lambda b, g, i: b * nq + i
    return pl.pallas_call(
        functools.partial(_diff_prompt_body, tq=tq, tk=tk, seq=seq),
        out_shape=jax.ShapeDtypeStruct((batch * seq, 512), BF16),
        grid=(batch, DF_KV, nq),
        in_specs=[_smem(),
                  pl.BlockSpec((tq, 256), lambda b, g, i: (row(b, g, i), UB_DFQ // 256 + g)),
                  pl.BlockSpec((seq, 256), lambda b, g, i: (b, KV_DIFF // 256 + g)),
                  pl.BlockSpec((tq, 256), lambda b, g, i: (row(b, g, i), UB_DFZ // 256 + g)),
                  pl.BlockSpec((1, DF_VD), lambda b, g, i: (0, 0))],
        out_specs=pl.BlockSpec((tq, 256), lambda b, g, i: (row(b, g, i), g)),
        scratch_shapes=[pltpu.VMEM((seq, 128), BF16), pltpu.VMEM((seq // tk, 128, tk), BF16)],
        compiler_params=_params("arbitrary", "arbitrary", "arbitrary"),
        name="diff_prompt",
    )(sc, ub, ukv, ub, nw)


def _diff_sample_body(pt_ref, sc_ref, q_ref, new_ref, z_ref, nw_ref, *rest, PP, past, n_tok):
    del pt_ref
    pages = rest[:PP]
    a_ref = rest[PP]
    m_s, l_s, acc_s = rest[PP + 1:]
    c = pl.program_id(1)
    R = DF_HEADS // DF_KV
    rows = 2 * R * SUB

    @pl.when(c == 0)
    def _():
        m_s[...] = jnp.full(m_s.shape, NEG, F32)
        l_s[...] = jnp.zeros(l_s.shape, F32)
        acc_s[...] = jnp.zeros(acc_s.shape, F32)

    lane = _iota((SUB, LANES), 1)
    ridx = _iota((rows, 1), 0)
    tok = ridx & (SUB - 1)
    head = (ridx >> 3) & (R - 1)
    qpos = (past + tok).astype(F32)

    def make_q(g):
        pieces = []
        for m in range(2):
            for r in range(R):
                qs = _scaled_q(q_ref[:, g * 256 + r * 128:g * 256 + (r + 1) * 128], DF_HD ** -0.5)
                pieces.append(_half_mask(qs, lane, m == 1))
        slope = jnp.where(head == 0, sc_ref[2 + g * R], sc_ref[2 + g * R + 1])
        return jnp.concatenate(pieces, axis=0), slope

    def page_rows(p, g, j):
        return pages[p][pl.ds(2 * j + g, PAGE_SIZE, stride=2 * DF_KV), :].astype(BF16)

    for g in range(DF_KV):
        qg, slope = make_q(g)
        s = jnp.concatenate([_dot_nt(qg, page_rows(p, g, 0)) for p in range(PP)], axis=1)
        kpos = (c * (PP * PAGE_SIZE) + _iota((rows, PP * PAGE_SIZE), 1)).astype(F32)
        s = s - slope * (qpos - kpos)

        def pv(p):
            out = _dot(p[:, 0:PAGE_SIZE], page_rows(0, g, 1))
            for j in range(1, PP):
                out += _dot(p[:, j * PAGE_SIZE:(j + 1) * PAGE_SIZE], page_rows(j, g, 1))
            return out

        m, l, acc = _softmax_update((m_s[g], l_s[g], acc_s[g]), s, pv)
        m_s[g], l_s[g], acc_s[g] = m, l, acc

    @pl.when(c == pl.num_programs(1) - 1)
    def _():
        lam, oml = sc_ref[0], sc_ref[1]
        pad = jnp.zeros((LANES - SUB, LANES), F32)
        jcol = _iota((rows, LANES), 1)
        for g in range(DF_KV):
            qg, slope = make_q(g)
            ks = g * 256
            knew = jnp.concatenate([new_ref[:, ks:ks + 128], pad], axis=0).astype(BF16)
            vnew = jnp.concatenate([new_ref[:, ks + 128:ks + 256], pad], axis=0).astype(BF16)
            dist = (tok - jcol).astype(F32)
            s = jnp.where((jcol <= tok) & (jcol < n_tok), _dot_nt(qg, knew) - slope * dist, NEG)
            _, l, acc = _softmax_update((m_s[g], l_s[g], acc_s[g]), s, lambda p: _dot(p, vnew))
            o = acc / l
            for r in range(R):
                o_r = jnp.concatenate([o[r * SUB:(r + 1) * SUB], o[(R + r) * SUB:(R + r + 1) * SUB]], axis=0)
                cs = slice(g * 256 + r * 128, g * 256 + (r + 1) * 128)
                a_ref[:, cs] = _diff_finish(o_r, lam, oml, nw_ref[...], z_ref[:, cs], SUB)


def _diff_sample(l, sc, pt_flat, ub, ukv, nw, cache, batch, n_pages, n_phys, n_tok):
    PP = min(16, n_pages)
    past = n_pages * PAGE_SIZE
    R = DF_HEADS // DF_KV
    rows = 2 * R * SUB

    def page_spec(p):
        return pl.BlockSpec((None, 4 * PAGE_SIZE, LANES),
                            lambda b, c, pt: (l * n_phys + pt[b * n_pages + c * PP + p], 0, 0))

    grid_spec = pltpu.PrefetchScalarGridSpec(
        num_scalar_prefetch=1,
        grid=(batch, n_pages // PP),
        in_specs=[_smem(),
                  pl.BlockSpec((SUB, 512), lambda b, c, pt: (b, UB_DFQ // 512)),
                  pl.BlockSpec((SUB, 512), lambda b, c, pt: (b, KV_DIFF // 512)),
                  pl.BlockSpec((SUB, 512), lambda b, c, pt: (b, UB_DFZ // 512)),
                  pl.BlockSpec((1, DF_VD), lambda b, c, pt: (0, 0))] + [page_spec(p) for p in range(PP)],
        out_specs=pl.BlockSpec((SUB, 512), lambda b, c, pt: (b, 0)),
        scratch_shapes=[pltpu.VMEM((DF_KV, rows, 1), F32), pltpu.VMEM((DF_KV, rows, 1), F32),
                        pltpu.VMEM((DF_KV, rows, DF_VD), F32)])
    return pl.pallas_call(
        functools.partial(_diff_sample_body, PP=PP, past=past, n_tok=n_tok),
        out_shape=jax.ShapeDtypeStruct((batch * SUB, 512), BF16),
        grid_spec=grid_spec,
        compiler_params=_params("arbitrary", "arbitrary"),
        name="diff_sample",
    )(pt_flat, sc, ub, ukv, ub, nw, *([cache] * PP))


def _block_sum(rows_ref, r0, cw):
    return jnp.sum(rows_ref[r0:r0 + CMP_BLK, 0:256] * cw, axis=0, keepdims=True)


def _top_blocks(imp, rowq, lane, n_blocks, n_pick):
    jq = rowq >> 6
    forced = (lane == 0) | (lane == jq) | (lane == jq - 1)
    score = jnp.where((lane <= jq) & (lane < n_blocks), imp + jnp.where(forced, FORCE, 0.0), NEG)
    lanef = lane.astype(F32)
    sel = jnp.zeros(imp.shape, F32)
    picks = []
    for _ in range(n_pick):
        mx = jnp.max(score, axis=-1, keepdims=True)
        am = jnp.min(jnp.where(score == mx, lanef, float(LANES)), axis=-1, keepdims=True)
        hit = lanef == am
        picked = mx > 0.5 * NEG
        sel = jnp.where(hit, jnp.where(picked, 1.0, 0.0), sel)
        score = jnp.where(hit, REMOVED, score)
        picks.append(jnp.where(picked, am, -1.0))
    return sel, picks


def _rank_select(score, n_blocks, n_pick):
    nv = score.shape[0] // SUB
    parts = [score[v * SUB:(v + 1) * SUB] for v in range(nv)]
    rank = [jnp.zeros(parts[0].shape, F32) for _ in range(nv)]
    rows = _iota(parts[0].shape, 0)
    for j in range(n_blocks):
        row = score[j:j + 1, :]
        for v in range(nv):
            if v * SUB > j:
                beats = jnp.where(row >= parts[v], 1.0, 0.0)
            elif v * SUB + SUB - 1 <= j:
                beats = jnp.where(row > parts[v], 1.0, 0.0)
            else:
                beats = jnp.where(rows + v * SUB > j, jnp.where(row >= parts[v], 1.0, 0.0),
                                  jnp.where(row > parts[v], 1.0, 0.0))
            rank[v] = rank[v] + beats
    return jnp.concatenate(
        [jnp.where((rank[v] < n_pick) & (parts[v] > 0.5 * NEG), 1.0, 0.0) for v in range(nv)], axis=0)


def _nsa_prompt_body(sl_ref, q_ref, gn_ref, z_ref, nsa_ref, win_ref, cw_ref, a_ref,
                     kce, kco, vce, vco, ksb, vst, kwb, vwt, *, tq, tk, seq):
    i = pl.program_id(1)
    nb = seq // SEL_BLK
    nbp = -(-nb // SUB) * SUB

    @pl.when(i == 0)
    def _():
        ksb[...] = nsa_ref[:, 256:384].astype(BF16)
        kwb[...] = win_ref[:, 0:128].astype(BF16)
        _fill_transposed(vst, nsa_ref, slice(384, 512), seq, tk)
        _fill_transposed(vwt, win_ref, slice(128, 256), seq, tk)
        kce[...] = jnp.zeros(kce.shape, F32)
        kco[...] = jnp.zeros(kco.shape, F32)
        cw = cw_ref[...]
        rows8 = _iota((SUB, 256), 0)
        v_even, v_odd = [], []
        for t8 in range(nbp // SUB):
            te = jnp.zeros((SUB, 256), F32)
            to = jnp.zeros((SUB, 256), F32)
            for e in range(SUB):
                j = t8 * SUB + e
                if j < nb:
                    te = jnp.where(rows8 == e, _block_sum(nsa_ref, 2 * j * CMP_BLK, cw), te)
                    to = jnp.where(rows8 == e, _block_sum(nsa_ref, (2 * j + 1) * CMP_BLK, cw), to)
            rs = slice(t8 * SUB, (t8 + 1) * SUB)
            kce[rs, :], kco[rs, :] = te[:, :128], to[:, :128]
            v_even.append(te[:, 128:])
            v_odd.append(to[:, 128:])
        pad = [jnp.zeros((LANES - nbp, LANES), F32)] if nbp < LANES else []
        vce[...] = jnp.concatenate(v_even + pad, axis=0).T
        vco[...] = jnp.concatenate(v_odd + pad, axis=0).T

    H = NS_KV * NS_R
    lane = _iota((tq, LANES), 1)
    blk = _iota((nbp, tq), 0)
    qpos = _iota((nbp, tq), 1) + i * tq
    qposf = qpos.astype(F32)
    kc = (kce[0:nbp, :].astype(BF16), kco[0:nbp, :].astype(BF16))
    vct = (vce[:, 0:nbp].astype(BF16), vco[:, 0:nbp].astype(BF16))
    cend = (blk * SEL_BLK + (CMP_BLK - 1), blk * SEL_BLK + (SEL_BLK - 1))
    valid = tuple((qpos >= ce) & (blk < nb) for ce in cend)
    qp = [_scaled_q(q_ref[:, r * 128:(r + 1) * 128], NS_HD ** -0.5 * LOG2E) for r in range(NS_R)]
    slopes = [sl_ref[h] * LOG2E for h in range(H)]
    qms, o_cmp, sel_t = [], [], []
    for g in range(NS_KV):
        imp = jnp.zeros((nbp, tq), F32)
        for r in range(NS_R):
            qm = _half_mask(qp[r], lane, g == 1)
            slope = slopes[g * NS_R + r]
            s = [jnp.where(valid[par], _dot_nt(kc[par], qm) - slope * (qposf - cend[par].astype(F32)), NEG)
                 for par in range(2)]
            mx = jnp.maximum(jnp.max(s[0], axis=0, keepdims=True), jnp.max(s[1], axis=0, keepdims=True))
            e = [jnp.where(valid[par], jnp.exp2(s[par] - mx), 0.0) for par in range(2)]
            den = jnp.maximum(jnp.sum(e[0], axis=0, keepdims=True) + jnp.sum(e[1], axis=0, keepdims=True), TINY)
            p = [e[0] / den, e[1] / den]
            qms.append(qm)
            o_cmp.append(_dot(vct[0], p[0].astype(BF16)) + _dot(vct[1], p[1].astype(BF16)))
            imp = imp + p[0] + p[1]
        jq = qpos >> 6
        forced = (blk == 0) | (blk == jq) | (blk == jq - 1)
        score = jnp.where((blk <= jq) & (blk < nb), imp + jnp.where(forced, FORCE, 0.0), NEG)
        sel_t.append(_rank_select(score, nb, SEL_N).astype(BF16))

    rel = _rel_t(tk, tq)
    nrel = [-slopes[h] * rel for h in range(H)]
    srow8 = jnp.concatenate([jnp.full((1, tq), slopes[h], F32) for h in range(H)], axis=1)

    def attend(kt, carry, k_ref, vt_ref, bias_fn):
        off = pl.multiple_of(kt * tk, tk)
        d0 = (i * tq - kt * tk).astype(F32)
        bias = bias_fn(kt, rel + d0)
        s = _dot_nt(k_ref[pl.ds(off, tk), :], q8) + jnp.concatenate(
            [nrel[h] + bias[h // NS_R] for h in range(H)], axis=1)
        return _tflash_update(carry, s, -d0 * srow8, vt_ref[kt])

    def sel_bias(kt, dist):
        kblk = (_iota((tk, nbp), 0) + kt * tk) >> 6
        expand = jnp.where(_iota((tk, nbp), 1) == kblk, 1.0, 0.0).astype(BF16)
        return [jnp.where((_dot(expand, st) > 0.5) & (dist >= 0), 0.0, NEG) for st in sel_t]

    def win_bias(kt, dist):
        return [jnp.where((dist >= 0) & (dist <= WINDOW), 0.0, NEG)] * NS_KV

    n_t = ((i + 1) * tq + tk - 1) // tk
    q8 = jnp.concatenate(qms, axis=0)
    _, l_sel, a_sel = lax.fori_loop(0, n_t, lambda kt, c: attend(kt, c, ksb, vst, sel_bias), _tflash_init(H * tq))
    _, l_win, a_win = lax.fori_loop(jnp.maximum(i * tq - WINDOW, 0) // tk, n_t,
                                    lambda kt, c: attend(kt, c, kwb, vwt, win_bias), _tflash_init(H * tq))
    o_sel, o_win = a_sel / l_sel, a_win / l_win
    gates_t = _sigmoid(gn_ref[...]).T
    half = _iota((LANES, tq), 0) < NS_HD
    for r in range(NS_R):
        og = []
        for g in range(NS_KV):
            h = g * NS_R + r
            cs = slice(h * tq, (h + 1) * tq)
            og.append(gates_t[3 * h:3 * h + 1, :] * o_cmp[h] + gates_t[3 * h + 1:3 * h + 2, :] * o_sel[:, cs]
                      + gates_t[3 * h + 2:3 * h + 3, :] * o_win[:, cs])
        o = jnp.where(half, og[0], og[1]).T
        cs = slice(r * 128, (r + 1) * 128)
        a_ref[:, cs] = (o * _silu(z_ref[:, cs].astype(F32))).astype(BF16)


def _nsa_prompt(sl, ub, uf, ukv, cw, batch, seq):
    tq = LANES
    tk = min(256, seq)
    nq = seq // tq
    row = lambda b, i: b * nq + i
    return pl.pallas_call(
        functools.partial(_nsa_prompt_body, tq=tq, tk=tk, seq=seq),
        out_shape=jax.ShapeDtypeStruct((batch * seq, 512), BF16),
        grid=(batch, nq),
        in_specs=[_smem(),
                  pl.BlockSpec((tq, 512), lambda b, i: (row(b, i), UB_NSQ // 512)),
                  pl.BlockSpec((tq, 128), lambda b, i: (row(b, i), UF_NSG // 128)),
                  pl.BlockSpec((tq, 512), lambda b, i: (row(b, i), UB_NSZ // 512)),
                  pl.BlockSpec((seq, 512), lambda b, i: (b, KV_NSA // 512)),
                  pl.BlockSpec((seq, 256), lambda b, i: (b, KV_WIN // 256)),
                  pl.BlockSpec((CMP_BLK, 256), lambda b, i: (0, 0))],
        out_specs=pl.BlockSpec((tq, 512), lambda b, i: (row(b, i), 0)),
        scratch_shapes=([pltpu.VMEM((LANES, 128), F32)] * 4
                        + [pltpu.VMEM((seq, 128), BF16), pltpu.VMEM((seq // tk, 128, tk), BF16)] * 2),
        compiler_params=_params("arbitrary", "arbitrary"),
        name="nsa_prompt",
    )(sl, ub, uf, ub, ukv, ukv, cw)


def _nsa_sample_a_body(pt_ref, sl_ref, q_ref, gn_ref, buf_ref, wnew_ref, wk_ref, wv_ref, *rest,
                       PP, past, n_tok, n_groups):
    del pt_ref
    pages = rest[:PP]
    part_ref, idx_ref = rest[PP:PP + 2]
    kct, vct = rest[PP + 2:]
    c = pl.program_id(1)
    nsb = past // SEL_BLK

    @pl.when(c == 0)
    def _():
        kct[...] = jnp.zeros(kct.shape, F32)
        vct[...] = jnp.zeros(vct.shape, F32)

    grp = (c * PP) // CMP_PAGES
    s0 = (c * PP) % CMP_PAGES
    acc_k = jnp.zeros((LANES, LANES), F32)
    acc_v = jnp.zeros((LANES, LANES), F32)
    for p in range(PP):
        acc_k += _dot(pages[p][0:128, :].astype(BF16), wk_ref[s0 + p])
        acc_v += _dot(pages[p][128:256, :].astype(BF16), wv_ref[s0 + p])
    kct[grp] = kct[grp] + acc_k
    vct[grp] = vct[grp] + acc_v

    @pl.when(c == pl.num_programs(1) - 1)
    def _():
        lane = _iota((SUB, LANES), 1)
        rowt = _iota((SUB, LANES), 0)
        rowq = rowt + past
        qposf = rowq.astype(F32)
        kc = [kct[G].astype(BF16) for G in range(n_groups)]
        vc = [vct[G].astype(BF16) for G in range(n_groups)]
        cend, valid = [], []
        for G in range(n_groups):
            blk = G * (LANES // 2) + (lane & (LANES // 2 - 1))
            ce = blk * SEL_BLK + (CMP_BLK - 1) + (lane >> 6) * CMP_BLK
            cend.append(ce)
            valid.append((rowq >= ce) & (blk < nsb))
        gates = _sigmoid(gn_ref[...])
        nbuf = buf_ref.shape[1]
        kw = buf_ref[0:128, :].astype(BF16)
        vw = buf_ref[128:256, :].astype(BF16)
        pad = jnp.zeros((LANES - SUB, LANES), F32)
        kwn = jnp.concatenate([wnew_ref[:, 0:128], pad], axis=0).astype(BF16)
        vwn = jnp.concatenate([wnew_ref[:, 128:256], pad], axis=0).astype(BF16)
        rows4 = NS_R * SUB
        tok4 = _iota((rows4, 1), 0) & (SUB - 1)
        head4 = _iota((rows4, 1), 0) >> 3
        outs = []
        for g in range(NS_KV):
            qms, o_cmp = [], []
            imps = [jnp.zeros((SUB, LANES), F32) for _ in range(n_groups)]
            for r in range(NS_R):
                qm = _half_mask(_scaled_q(q_ref[:, r * 128:(r + 1) * 128], NS_HD ** -0.5), lane, g == 1)
                slope = sl_ref[g * NS_R + r]
                s = [jnp.where(valid[G], _dot(qm, kc[G]) - slope * (qposf - cend[G].astype(F32)), NEG)
                     for G in range(n_groups)]
                mx = functools.reduce(jnp.maximum, [jnp.max(x, axis=-1, keepdims=True) for x in s])
                e = [jnp.where(valid[G], jnp.exp(s[G] - mx), 0.0) for G in range(n_groups)]
                den = jnp.maximum(sum(jnp.sum(x, axis=-1, keepdims=True) for x in e), TINY)
                o = jnp.zeros((SUB, LANES), F32)
                for G in range(n_groups):
                    p = e[G] / den
                    o = o + _dot_nt(p.astype(BF16), vc[G])
                    imps[G] = imps[G] + p
                qms.append(qm)
                o_cmp.append(o)
            imps = [x + pltpu.roll(x, LANES // 2, 1) for x in imps]
            imp = imps[0] if n_groups == 1 else jnp.where(lane < LANES // 2, imps[0], imps[1])
            _, picks = _top_blocks(imp, rowq, lane, nsb, SEL_N - 1)
            idx = jnp.full((SUB, LANES), -1.0, F32)
            for kk, pk in enumerate(picks):
                idx = jnp.where(lane == kk, pk, idx)
            idx_ref[g] = idx.astype(jnp.int32)
            q4 = jnp.concatenate(qms, axis=0)
            slope4 = jnp.zeros((rows4, 1), F32)
            for r in range(NS_R):
                slope4 = jnp.where(head4 == r, sl_ref[g * NS_R + r], slope4)
            dist_b = tok4 + (nbuf - _iota((rows4, nbuf), 1))
            s_b = jnp.where((dist_b >= 0) & (dist_b <= WINDOW),
                            _dot(q4, kw) - slope4 * dist_b.astype(F32), NEG)
            jn = _iota((rows4, LANES), 1)
            s_n = jnp.where((jn <= tok4) & (jn < n_tok), _dot_nt(q4, kwn) - slope4 * (tok4 - jn).astype(F32), NEG)
            mx = jnp.maximum(jnp.max(s_b, axis=-1, keepdims=True), jnp.max(s_n, axis=-1, keepdims=True))
            p_b, p_n = jnp.exp(s_b - mx), jnp.exp(s_n - mx)
            den = jnp.sum(p_b, axis=-1, keepdims=True) + jnp.sum(p_n, axis=-1, keepdims=True)
            o_win = (_dot_nt(p_b.astype(BF16), vw) + _dot(p_n.astype(BF16), vwn)) / den
            og = []
            for r in range(NS_R):
                col = (g * NS_R + r) * 3
                og.append(gates[:, col:col + 1] * o_cmp[r]
                          + gates[:, col + 2:col + 3] * o_win[r * SUB:(r + 1) * SUB])
            outs.append(og)
        for r in range(NS_R):
            part_ref[:, r * 128:(r + 1) * 128] = jnp.where(lane < 64, outs[0][r], outs[1][r])


def _nsa_sample_a(l, sl, pt_flat, ub, uf, ukv, buf_t, wk, wv, cache_t, batch, n_pages, n_phys, n_tok):
    PP = min(16, n_pages)
    past = n_pages * PAGE_SIZE
    nbuf = buf_t.shape[2]
    n_groups = -(-n_pages // CMP_PAGES)
    assert CMP_PAGES % PP == 0

    def page_spec(p):
        return pl.BlockSpec((None, 256, PAGE_SIZE),
                            lambda b, c, pt: (l * n_phys + pt[b * n_pages + c * PP + p], 0, 0))

    grid_spec = pltpu.PrefetchScalarGridSpec(
        num_scalar_prefetch=1,
        grid=(batch, n_pages // PP),
        in_specs=[_smem(),
                  pl.BlockSpec((SUB, 512), lambda b, c, pt: (b, UB_NSQ // 512)),
                  pl.BlockSpec((SUB, 128), lambda b, c, pt: (b, UF_NSG // 128)),
                  pl.BlockSpec((None, 256, nbuf), lambda b, c, pt: (l * batch + b, 0, 0)),
                  pl.BlockSpec((SUB, 256), lambda b, c, pt: (b, KV_WIN // 256)),
                  pl.BlockSpec((CMP_PAGES, PAGE_SIZE, LANES), lambda b, c, pt: (0, 0, 0)),
                  pl.BlockSpec((CMP_PAGES, PAGE_SIZE, LANES), lambda b, c, pt: (0, 0, 0))]
                 + [page_spec(p) for p in range(PP)],
        out_specs=(pl.BlockSpec((SUB, 512), lambda b, c, pt: (b, 0)),
                   pl.BlockSpec((None, NS_KV, SUB, LANES), lambda b, c, pt: (b, 0, 0, 0))),
        scratch_shapes=[pltpu.VMEM((n_groups, LANES, LANES), F32)] * 2)
    return pl.pallas_call(
        functools.partial(_nsa_sample_a_body, PP=PP, past=past, n_tok=n_tok, n_groups=n_groups),
        out_shape=(jax.ShapeDtypeStruct((batch * SUB, 512), F32),
                   jax.ShapeDtypeStruct((batch, NS_KV, SUB, LANES), jnp.int32)),
        grid_spec=grid_spec,
        compiler_params=_params("arbitrary", "arbitrary"),
        name="nsa_sample_a",
    )(pt_flat, sl, ub, uf, buf_t, ukv, wk, wv, *([cache_t] * PP))


def _nsa_sample_b_body(pt_ref, idx_ref, sl_ref, q_ref, gn_ref, z_ref, new_ref, part_ref, *rest, NSEL, past, n_tok):
    del pt_ref
    blocks = rest[:NSEL]
    a_ref = rest[NSEL]
    acc_scr = rest[NSEL + 1]
    b, g, t = pl.program_id(0), pl.program_id(1), pl.program_id(2)

    @pl.when((g == 0) & (t == 0))
    def _():
        acc_scr[...] = part_ref[...]

    lane = _iota((SUB, LANES), 1)
    rows = _iota((SUB, LANES), 0)
    ghalf = (lane >> 6) == g
    rcol = _iota((SUB, 1), 0)
    gates = _sigmoid(gn_ref[...])
    q = jnp.zeros((SUB, LANES), F32)
    slope = jnp.zeros((SUB, 1), F32)
    gate = jnp.zeros((SUB, 1), F32)
    for r in range(NS_R):
        qp = q_ref[:, r * 128:(r + 1) * 128].astype(F32)
        q = jnp.where(rows == r, jnp.sum(jnp.where(rows == t, qp, 0.0), axis=0, keepdims=True), q)
        slope = jnp.where(rcol == r, sl_ref[g * NS_R + r], slope)
        gv = jnp.where((rows == t) & (lane == (g * NS_R + r) * 3 + 1), gates, 0.0)
        gv = jnp.sum(jnp.sum(gv, axis=1, keepdims=True), axis=0, keepdims=True)
        gate = jnp.where(rcol == r, gv, gate)
    qb = jnp.where(ghalf, q * NS_HD ** -0.5, 0.0).astype(BF16)
    qpos = past + t
    base = ((b * NS_KV + g) * n_tok + t) * SEL_N
    s_all = []
    for k in range(NSEL):
        blk = idx_ref[base + k]
        half = jnp.where(blk >= 0, blk & 1, 2)
        kpos = (blk >> 1) * PAGE_SIZE + lane
        s = _dot(qb, blocks[k][0:128, :].astype(BF16)) - slope * (qpos - kpos).astype(F32)
        s_all.append(jnp.where((lane >> 6) == half, s, NEG))
    pad = jnp.zeros((LANES - SUB, LANES), F32)
    knew = jnp.concatenate([new_ref[:, 0:128], pad], axis=0).astype(BF16)
    vnew = jnp.concatenate([new_ref[:, 128:256], pad], axis=0).astype(BF16)
    s = _dot_nt(qb, knew) - slope * (t - lane).astype(F32)
    s_all.append(jnp.where((lane <= t) & (lane < n_tok), s, NEG))
    mx = functools.reduce(jnp.maximum, [x.max(axis=-1, keepdims=True) for x in s_all])
    den = jnp.zeros((SUB, 1), F32)
    o = jnp.zeros((SUB, LANES), F32)
    for k, s in enumerate(s_all):
        p = jnp.exp(s - mx)
        den = den + jnp.sum(p, axis=-1, keepdims=True)
        pb = p.astype(BF16)
        o = o + (_dot_nt(pb, blocks[k][128:256, :].astype(BF16)) if k < NSEL else _dot(pb, vnew))
    o = gate * o / den
    for r in range(NS_R):
        cs = slice(r * 128, (r + 1) * 128)
        acc_scr[:, cs] = acc_scr[:, cs] + jnp.where((rows == t) & ghalf, o[r:r + 1, :], 0.0)

    @pl.when((g == NS_KV - 1) & (t == n_tok - 1))
    def _():
        a_ref[...] = (acc_scr[...] * _silu(z_ref[...].astype(F32))).astype(BF16)


def _nsa_sample_b(l, sl, pt_flat, idx_flat, ub, uf, ukv, part, cache_t, batch, n_pages, n_phys, n_tok):
    NSEL = SEL_N - 1
    past = n_pages * PAGE_SIZE
    halves = PAGE_SIZE // SEL_BLK

    def blk_spec(k):
        def index(b, g, t, pt, idx):
            blk = jnp.maximum(idx[((b * NS_KV + g) * n_tok + t) * SEL_N + k], 0)
            return (l * n_phys + pt[b * n_pages + blk // halves], 1, 0)
        return pl.BlockSpec((None, 256, PAGE_SIZE), index)

    grid_spec = pltpu.PrefetchScalarGridSpec(
        num_scalar_prefetch=2,
        grid=(batch, NS_KV, n_tok),
        in_specs=[_smem(),
                  pl.BlockSpec((SUB, 512), lambda b, g, t, pt, idx: (b, UB_NSQ // 512)),
                  pl.BlockSpec((SUB, 128), lambda b, g, t, pt, idx: (b, UF_NSG // 128)),
                  pl.BlockSpec((SUB, 512), lambda b, g, t, pt, idx: (b, UB_NSZ // 512)),
                  pl.BlockSpec((SUB, 256), lambda b, g, t, pt, idx: (b, (KV_NSA + 256) // 256)),
                  pl.BlockSpec((SUB, 512), lambda b, g, t, pt, idx: (b, 0))] + [blk_spec(k) for k in range(NSEL)],
        out_specs=pl.BlockSpec((SUB, 512), lambda b, g, t, pt, idx: (b, 0)),
        scratch_shapes=[pltpu.VMEM((SUB, 512), F32)])
    return pl.pallas_call(
        functools.partial(_nsa_sample_b_body, NSEL=NSEL, past=past, n_tok=n_tok),
        out_shape=jax.ShapeDtypeStruct((batch * SUB, 512), BF16),
        grid_spec=grid_spec,
        compiler_params=_params("arbitrary", "arbitrary", "arbitrary"),
        name="nsa_sample_b",
    )(pt_flat, idx_flat, sl, ub, uf, ub, ukv, part, *([cache_t] * NSEL))


def _final_body(x_ref, gate_ref, gp_ref, ahg_ref, adf_ref, ans_ref, mg_ref, whg_ref, wdf_ref, wns_ref, wo_ref, o_ref):
    d = x_ref.shape[-1]
    merged = (_sigmoid(mg_ref[:, 0:d].astype(F32)) * _dot(ahg_ref[...], whg_ref[...])
              + _sigmoid(mg_ref[:, d:2 * d].astype(F32)) * _dot(adf_ref[...], wdf_ref[...])
              + _sigmoid(mg_ref[:, 2 * d:3 * d].astype(F32)) * _dot(ans_ref[...], wns_ref[...]))
    out = _dot(merged.astype(BF16), wo_ref[...])
    ms = jnp.mean(out * out, axis=-1, keepdims=True)
    o_ref[...] = x_ref[...] + gate_ref[...] * (out * lax.rsqrt(ms + EPS) * gp_ref[...])


def _final(x2, cond, l, g_post, a_hg, a_df, a_ns, ub, w_hg, w_df, w_ns, w_o, tm, rows_per_batch, name):
    rows, d = x2.shape
    wspec = lambda k: pl.BlockSpec((None, k, d), lambda i: (l, 0, 0))
    return pl.pallas_call(
        _final_body,
        out_shape=jax.ShapeDtypeStruct((rows, d), F32),
        grid=(rows // tm,),
        in_specs=[pl.BlockSpec((tm, d), lambda i: (i, 0)),
                  _cond_spec(cond, l, 2, tm, rows_per_batch),
                  pl.BlockSpec((None, 1, d), lambda i: (l, 0, 0)),
                  pl.BlockSpec((tm, 512), lambda i: (i, 0)),
                  pl.BlockSpec((tm, 512), lambda i: (i, 0)),
                  pl.BlockSpec((tm, 512), lambda i: (i, 0)),
                  pl.BlockSpec((tm, 3 * d), lambda i: (i, UB_MG // (3 * d))),
                  wspec(512), wspec(512), wspec(512), wspec(d)],
        out_specs=pl.BlockSpec((tm, d), lambda i: (i, 0)),
        compiler_params=_params("arbitrary"),
        name=name,
    )(x2, cond, g_post, a_hg, a_df, a_ns, ub, w_hg, w_df, w_ns, w_o)


def _alibi(n):
    return [2.0 ** (-8.0 * (h + 1) / n) for h in range(n)]


def kernel(x_prompt, x_sample, cache_diff, cache_nsa, cache_nsa_win, state_hgrn, page_table, c_prompt, c_sample,
           w_cond, b_cond, g_pre, g_post, w_in, hg_lb, hg_norm, df_lam, df_norm, ns_cmp,
           w_hg_out, w_df_out, w_ns_out, w_out):
    depth, d, _ = w_in.shape
    bp, seq, _ = x_prompt.shape
    bs, n_tok, _ = x_sample.shape
    n_pages = page_table.shape[1]
    n_phys = cache_diff.shape[1]
    past = n_pages * PAGE_SIZE
    assert d == 1024 and seq % 128 == 0 and n_tok <= SUB and n_pages % 4 == 0
    assert cache_nsa_win.shape[2] == WINDOW and past >= WINDOW and past // SEL_BLK <= LANES

    kv_cols, uf_cols, ub_cols, pair = _column_layout()
    w_kv = jnp.take(w_in, kv_cols, axis=2).astype(BF16)
    w_uf = jnp.where(uf_cols >= 0, jnp.take(w_in, np.maximum(uf_cols, 0), axis=2), 0.0).astype(BF16)
    w_ub = jnp.take(w_in, ub_cols, axis=2).astype(BF16)
    w_hg = w_hg_out.astype(BF16)
    w_df = w_df_out.astype(BF16)
    w_ns = jnp.take(w_ns_out, pair, axis=1).astype(BF16)
    w_o = w_out.astype(BF16)
    g_pre3 = g_pre.reshape(depth, 1, d)
    g_post3 = g_post.reshape(depth, 1, d)
    lb_w = jax.nn.softmax(hg_lb.astype(F32), axis=0)
    lower = jnp.cumsum(lb_w, axis=0) - lb_w[0]
    cw_all = jax.nn.softmax(ns_cmp.astype(F32), axis=-1)
    cw_tile = jnp.repeat(jnp.swapaxes(cw_all, 1, 2), 128, axis=2)
    lv = df_lam.astype(F32)
    lam_init = jnp.asarray([0.8 - 0.6 * math.exp(-0.3 * l) for l in range(depth)], F32)
    lam = jnp.exp(jnp.sum(lv[:, 0] * lv[:, 1], axis=-1)) - jnp.exp(jnp.sum(lv[:, 2] * lv[:, 3], axis=-1)) + lam_init
    df_sc = jnp.concatenate([lam[:, None], 1.0 - lam_init[:, None],
                             jnp.broadcast_to(jnp.asarray(_alibi(DF_HEADS), F32), (depth, DF_HEADS))], axis=1)
    ns_sl = jnp.asarray(_alibi(NS_HEADS), F32)

    bc = bp + bs
    bc_pad = -(-bc // SUB) * SUB
    c_all = jnp.concatenate([c_prompt, c_sample, jnp.zeros((bc_pad - bc, d), F32)], axis=0)
    cond = _cond_all(c_all, w_cond, b_cond)
    cond_p = cond[:, :bp].reshape(depth, bp, 1, 3 * d)
    cond_s = jnp.repeat(cond[:, bp:bc], SUB, axis=1)

    pt_flat = page_table.reshape(-1).astype(jnp.int32)
    cache_diff2 = cache_diff.reshape(depth, n_phys, PAGE_SIZE, DF_KV, 2, LANES).transpose(0, 1, 2, 4, 3, 5)
    cache_diff2 = cache_diff2.reshape(depth * n_phys, 4 * PAGE_SIZE, LANES)
    cache_nsa_t = cache_nsa.transpose(0, 1, 3, 4, 5, 2).reshape(depth * n_phys, 512, PAGE_SIZE)
    win_buf_t = cache_nsa_win.transpose(0, 1, 3, 4, 5, 2).reshape(depth * bs, 256, WINDOW)
    pos = np.arange(PAGE_SIZE)
    nn = pos // CMP_BLK
    tgt = (nn % 2)[None, :] * (LANES // 2) + 2 * np.arange(CMP_PAGES)[:, None] + (nn // 2)[None, :]
    place = jnp.asarray(tgt[:, :, None] == np.arange(LANES)[None, None, :], F32)
    cw_pos = cw_all[:, :, pos % CMP_BLK]
    w_cmp = (place[None, None] * cw_pos[:, :, None, :, None]).astype(BF16)

    xp = x_prompt.reshape(bp * seq, d)
    xs = jnp.pad(x_sample, ((0, 0), (0, SUB - n_tok), (0, 0))).reshape(bs * SUB, d)
    zeros_state = jnp.zeros((bp, HG_HEADS, HG_DK, HG_DV), F32)
    tm_p = min(1024, seq)
    tm_f = min(512, seq)
    rows_s = bs * SUB

    outs = {k: [] for k in ('dp', 'ds', 'np', 'ns', 'wp', 'ws', 'hp', 'hs')}
    for l in range(depth):
        lb = lower[l].reshape(1, 512)
        nw_hg = hg_norm[l].reshape(1, HG_DV)
        nw_df = df_norm[l].reshape(1, DF_VD)

        ukv = _inproj(xp, cond_p, l, g_pre3, w_kv, KV_W, F32, tm_p, seq, "inproj_kv_p")
        uf = _inproj(xp, cond_p, l, g_pre3, w_uf, UF_W, F32, tm_p, seq, "inproj_f_p")
        ub = _inproj(xp, cond_p, l, g_pre3, w_ub, 512, BF16, tm_p, seq, "inproj_b_p")
        a_hg, s_p = _hgrn_prompt(ub, uf, lb, nw_hg, zeros_state, bp, seq)
        a_df = _diff_prompt(df_sc[l], ub, ukv, nw_df, bp, seq)
        a_ns = _nsa_prompt(ns_sl, ub, uf, ukv, cw_tile[l], bp, seq)
        xp = _final(xp, cond_p, l, g_post3, a_hg, a_df, a_ns, ub, w_hg, w_df, w_ns, w_o, tm_f, seq, "final_p")
        ukv3 = ukv.reshape(bp, seq, KV_W)
        outs['dp'].append(ukv3[:, :, KV_DIFF:KV_NSA].reshape(bp, seq, DF_KV, 256))
        outs['np'].append(ukv3[:, :, KV_NSA:KV_WIN].reshape(bp, seq, 4, NS_KV, NS_HD))
        outs['wp'].append(ukv3[:, seq - min(WINDOW, seq):, KV_WIN:].reshape(bp, min(WINDOW, seq), 2, NS_KV, NS_HD))
        outs['hp'].append(s_p)

        ukv = _inproj(xs, cond_s, l, g_pre3, w_kv, KV_W, F32, rows_s, SUB, "inproj_kv_s")
        uf = _inproj(xs, cond_s, l, g_pre3, w_uf, UF_W, F32, rows_s, SUB, "inproj_f_s")
        ub = _inproj(xs, cond_s, l, g_pre3, w_ub, 512, BF16, rows_s, SUB, "inproj_b_s")
        a_hg, s_s = _hgrn_sample(ub, uf, lb, nw_hg, state_hgrn[l], bs, n_tok)
        a_df = _diff_sample(l, df_sc[l], pt_flat, ub, ukv, nw_df, cache_diff2, bs, n_pages, n_phys, n_tok)
        part, idx = _nsa_sample_a(l, ns_sl, pt_flat, ub, uf, ukv, win_buf_t, w_cmp[l, 0], w_cmp[l, 1], cache_nsa_t,
                                  bs, n_pages, n_phys, n_tok)
        idx_flat = idx[:, :, :n_tok, :SEL_N].reshape(-1)
        a_ns = _nsa_sample_b(l, ns_sl, pt_flat, idx_flat, ub, uf, ukv, part, cache_nsa_t,
                             bs, n_pages, n_phys, n_tok)
        xs = _final(xs, cond_s, l, g_post3, a_hg, a_df, a_ns, ub, w_hg, w_df, w_ns, w_o, rows_s, SUB, "final_s")
        ukv3 = ukv.reshape(bs, SUB, KV_W)[:, :n_tok]
        outs['ds'].append(ukv3[:, :, KV_DIFF:KV_NSA].reshape(bs, n_tok, DF_KV, 256))
        outs['ns'].append(ukv3[:, :, KV_NSA:KV_WIN].reshape(bs, n_tok, 4, NS_KV, NS_HD))
        new_w = ukv3[:, :, KV_WIN:].reshape(bs, n_tok, 2, NS_KV, NS_HD)
        outs['ws'].append(jnp.concatenate([cache_nsa_win[l][:, n_tok:], new_w], axis=1))
        outs['hs'].append(s_s)

    st = {k: jnp.stack(v) for k, v in outs.items()}
    y_p = xp.reshape(bp, seq, d)
    y_s = xs.reshape(bs, SUB, d)[:, :n_tok]
    return (y_p, y_s, st['dp'], st['ds'], st['np'], st['ns'], st['wp'], st['ws'], st['hp'], st['hs'])
```

```python
import functools
import math

import numpy as np
import jax
import jax.numpy as jnp
from jax import lax
from jax.experimental import pallas as pl
from jax.experimental.pallas import tpu as pltpu

F32 = jnp.float32
BF16 = jnp.bfloat16

PAGE_SIZE = 128
HG_HEADS, HG_DK, HG_DV = 4, 128, 128
HG_CHUNK = 64
DF_HEADS, DF_KV, DF_HD = 4, 2, 64
DF_VD = 2 * DF_HD
NS_HEADS, NS_KV, NS_HD = 8, 2, 64
NS_R = NS_HEADS // NS_KV
CMP_BLK, SEL_BLK, SEL_N, WINDOW = 32, 64, 8, 512
EPS, NEG, TINY, FORCE = 1e-6, -1e30, 1e-30, 1e4
REMOVED = -3e38
EXP_CLAMP = 80.0
LOG2E = math.log2(math.e)
LANES = 128
SUB = 8
CMP_PAGES = 32
VMEM_LIMIT = 48 * 1024 * 1024

IN_SPLITS = (
    ('hg_q', 512), ('hg_f', 512), ('hg_i', 512), ('hg_z', 512),
    ('df_q', 512), ('df_k', 256), ('df_v', 256), ('df_z', 512),
    ('ns_q', 512), ('ns_kvc', 256), ('ns_kvs', 256), ('ns_kvw', 256), ('ns_g', 24), ('ns_z', 512),
    ('merge', 3072),
)

KV_DIFF, KV_NSA, KV_WIN, KV_W = 0, 512, 1024, 1280
UF_HGF, UF_NSG, UF_W = 0, 512, 640
UB_MG, UB_HGQ, UB_HGI, UB_DFQ, UB_NSQ, UB_HGZ, UB_DFZ, UB_NSZ, UB_W = (
    0, 3072, 3584, 4096, 4608, 5120, 5632, 6144, 6656)


def _column_layout():
    off, o = {}, 0
    for n, w in IN_SPLITS:
        off[n] = o
        o += w
    rng = lambda n, a, b: list(range(off[n] + a, off[n] + b))
    kv = []
    for g in range(DF_KV):
        kv += rng('df_k', g * 128, (g + 1) * 128) + rng('df_v', g * 128, (g + 1) * 128)
    kv += rng('ns_kvc', 0, 256) + rng('ns_kvs', 0, 256) + rng('ns_kvw', 0, 256)
    uf = rng('hg_f', 0, 512) + rng('ns_g', 0, 24) + [-1] * 104
    pair = [g * 256 + r * 64 + d for r in range(NS_R) for g in range(NS_KV) for d in range(NS_HD)]
    ub = (rng('merge', 0, 3072) + rng('hg_q', 0, 512) + rng('hg_i', 0, 512) + rng('df_q', 0, 512)
          + [off['ns_q'] + p for p in pair] + rng('hg_z', 0, 512) + rng('df_z', 0, 512)
          + [off['ns_z'] + p for p in pair])
    assert len(kv) == KV_W and len(uf) == UF_W and len(ub) == UB_W
    return np.asarray(kv), np.asarray(uf), np.asarray(ub), np.asarray(pair)


def _dot(a, b):
    return jnp.dot(a, b, preferred_element_type=F32)


def _dot_nt(a, b):
    return lax.dot_general(a, b, (((1,), (1,)), ((), ())), preferred_element_type=F32)


def _sigmoid(x):
    return 1.0 / (1.0 + jnp.exp(-x))


def _silu(x):
    return x * _sigmoid(x)


def _iota(shape, dim):
    return lax.broadcasted_iota(jnp.int32, shape, dim)


def _params(*sem):
    return pltpu.CompilerParams(dimension_semantics=sem, vmem_limit_bytes=VMEM_LIMIT)


def _smem():
    return pl.BlockSpec(memory_space=pltpu.SMEM)


def _cond_body(c_ref, w_ref, b_ref, o_ref):
    a = _silu(c_ref[...])
    a_hi = a.astype(BF16)
    a_lo = (a - a_hi.astype(F32)).astype(BF16)
    w = w_ref[...]
    w_hi = w.astype(BF16)
    w_lo = (w - w_hi.astype(F32)).astype(BF16)
    o_ref[...] = _dot(a_hi, w_hi) + (_dot(a_hi, w_lo) + _dot(a_lo, w_hi)) + b_ref[...]


def _cond_all(c_all, w_cond, b_cond):
    depth, d, _ = w_cond.shape
    bc = c_all.shape[0]
    return pl.pallas_call(
        _cond_body,
        out_shape=jax.ShapeDtypeStruct((depth, bc, 3 * d), F32),
        grid=(depth, 3),
        in_specs=[pl.BlockSpec((bc, d), lambda l, k: (0, 0)),
                  pl.BlockSpec((None, d, d), lambda l, k: (l, 0, k)),
                  pl.BlockSpec((None, 1, d), lambda l, k: (l, 0, k))],
        out_specs=pl.BlockSpec((None, bc, d), lambda l, k: (l, 0, k)),
        compiler_params=_params("arbitrary", "arbitrary"),
        name="cond",
    )(c_all, w_cond, b_cond.reshape(depth, 1, 3 * d))


def _inproj_body(x_ref, sh_ref, sc_ref, gp_ref, w_ref, o_ref, h_scr):
    @pl.when(pl.program_id(1) == 0)
    def _():
        x = x_ref[...]
        ms = jnp.mean(x * x, axis=-1, keepdims=True)
        y = x * lax.rsqrt(ms + EPS) * gp_ref[...]
        h_scr[...] = (y * (1.0 + sc_ref[...]) + sh_ref[...]).astype(BF16)

    o_ref[...] = _dot(h_scr[...], w_ref[...]).astype(o_ref.dtype)


def _cond_spec(cond, l, k, tm, rows_per_batch):
    d = cond.shape[-1] // 3
    if cond.ndim == 4:
        tiles = rows_per_batch // tm
        return pl.BlockSpec((None, None, 1, d), lambda i, *_: (l, i // tiles, 0, k))
    return pl.BlockSpec((None, tm, d), lambda i, *_: (l, i, k))


def _inproj(x2, cond, l, g_pre, w, tn, out_dtype, tm, rows_per_batch, name):
    rows, d = x2.shape
    n = w.shape[-1]
    return pl.pallas_call(
        _inproj_body,
        out_shape=jax.ShapeDtypeStruct((rows, n), out_dtype),
        grid=(rows // tm, n // tn),
        in_specs=[pl.BlockSpec((tm, d), lambda i, j: (i, 0)),
                  _cond_spec(cond, l, 0, tm, rows_per_batch),
                  _cond_spec(cond, l, 1, tm, rows_per_batch),
                  pl.BlockSpec((None, 1, d), lambda i, j: (l, 0, 0)),
                  pl.BlockSpec((None, d, tn), lambda i, j: (l, 0, j))],
        out_specs=pl.BlockSpec((tm, tn), lambda i, j: (i, j)),
        scratch_shapes=[pltpu.VMEM((tm, d), BF16)],
        compiler_params=_params("arbitrary", "arbitrary"),
        name=name,
    )(x2, cond, cond, g_pre, w)


def _hgrn_gates(fr, lbh):
    e = jnp.exp(-jnp.abs(fr))
    inv = 1.0 / (1.0 + e)
    pos = fr >= 0
    sg = jnp.where(pos, inv, e * inv)
    sgn = jnp.where(pos, e * inv, inv)
    fg = lbh + (1.0 - lbh) * sg
    return fg, jnp.log(jnp.maximum(fg, TINY)), (1.0 - lbh) * sgn


def _hgrn_out(o, nw, z_ref, rs, hs, a_ref):
    ms = jnp.mean(o * o, axis=-1, keepdims=True)
    on = o * lax.rsqrt(ms + EPS) * nw
    a_ref[rs, hs] = (on * _silu(z_ref[rs, hs].astype(F32))).astype(BF16)


def _hgrn_chunk_body(q_ref, v_ref, f_ref, z_ref, lb_ref, nw_ref, s0_ref, a_ref, s_out_ref, s_scr, *, C, n_chunks):
    t = pl.program_id(1)

    @pl.when(t == 0)
    def _():
        s_scr[...] = s0_ref[...]

    ts = C * n_chunks
    shift = C.bit_length() - 1
    ri, ci = _iota((ts, ts), 0), _iota((ts, ts), 1)
    tri = (ri >= ci) & ((ri >> shift) == (ci >> shift))
    tri_b = jnp.where(tri, 1.0, 0.0).astype(BF16)
    mid = C // 2
    nw = nw_ref[...]
    _, g, k = _hgrn_gates(f_ref[...], lb_ref[...])
    q = _silu(q_ref[...].astype(F32))
    g1 = g.astype(BF16)
    r1 = g - g1.astype(F32)
    g2 = r1.astype(BF16)
    g3 = (r1 - g2.astype(F32)).astype(BF16)
    G = _dot(tri_b, g1) + _dot(tri_b, g2) + _dot(tri_b, g3)

    def per_chunk_row(idx):
        return jnp.concatenate(
            [jnp.broadcast_to(G[c * C + idx:c * C + idx + 1, :], (C, G.shape[1])) for c in range(n_chunks)], axis=0)

    g_mid, g_last = per_chunk_row(mid - 1), per_chunk_row(C - 1)
    qe = (q * jnp.exp(jnp.minimum(G - g_mid, EXP_CLAMP))).astype(BF16)
    ke = (k * jnp.exp(jnp.minimum(g_mid - G, EXP_CLAMP))).astype(BF16)
    qg = (q * jnp.exp(G)).astype(BF16)
    kd = k * jnp.exp(g_last - G)
    pad = jnp.zeros((LANES - C - SUB, HG_DK), F32)
    for h in range(HG_HEADS):
        hs = slice(h * HG_DK, (h + 1) * HG_DK)
        att = jnp.where(tri, _dot_nt(qe[:, hs], ke[:, hs]), 0.0)
        o_intra = _dot(att.astype(BF16), v_ref[:, hs])
        S = s_scr[h]
        o_inter = []
        for c in range(n_chunks):
            rs = slice(c * C, (c + 1) * C)
            o_inter.append(_dot(qg[rs, hs], S.astype(BF16)))
            decay = jnp.exp(G[(c + 1) * C - 1:(c + 1) * C, hs])
            tile_t = jnp.concatenate([kd[rs, hs], jnp.broadcast_to(decay, (SUB, HG_DK)), pad], axis=0).T
            S = tile_t[:, C:C + 1] * S + _dot(tile_t[:, :C].astype(BF16), v_ref[rs, hs])
        s_scr[h] = S
        _hgrn_out(o_intra + jnp.concatenate(o_inter, axis=0), nw, z_ref, slice(0, ts), hs, a_ref)

    @pl.when(t == pl.num_programs(1) - 1)
    def _():
        s_out_ref[...] = s_scr[...]


def _hgrn_prompt(ub, uf, lb, nw, s0, batch, seq):
    ts = min(256, seq)
    C = math.gcd(ts, HG_CHUNK)
    nt = seq // ts
    row = lambda b, t: b * nt + t
    return pl.pallas_call(
        functools.partial(_hgrn_chunk_body, C=C, n_chunks=ts // C),
        out_shape=(jax.ShapeDtypeStruct((batch * seq, 512), BF16),
                   jax.ShapeDtypeStruct((batch, HG_HEADS, HG_DK, HG_DV), F32)),
        grid=(batch, nt),
        in_specs=[pl.BlockSpec((ts, 512), lambda b, t: (row(b, t), UB_HGQ // 512)),
                  pl.BlockSpec((ts, 512), lambda b, t: (row(b, t), UB_HGI // 512)),
                  pl.BlockSpec((ts, 512), lambda b, t: (row(b, t), UF_HGF // 512)),
                  pl.BlockSpec((ts, 512), lambda b, t: (row(b, t), UB_HGZ // 512)),
                  pl.BlockSpec((1, 512), lambda b, t: (0, 0)),
                  pl.BlockSpec((1, HG_DV), lambda b, t: (0, 0)),
                  pl.BlockSpec((None, HG_HEADS, HG_DK, HG_DV), lambda b, t: (b, 0, 0, 0))],
        out_specs=(pl.BlockSpec((ts, 512), lambda b, t: (row(b, t), 0)),
                   pl.BlockSpec((None, HG_HEADS, HG_DK, HG_DV), lambda b, t: (b, 0, 0, 0))),
        scratch_shapes=[pltpu.VMEM((HG_HEADS, HG_DK, HG_DV), F32)],
        compiler_params=_params("arbitrary", "arbitrary"),
        name="hgrn_prompt",
    )(ub, ub, uf, ub, lb, nw, s0)


def _hgrn_step_body(q_ref, v_ref, f_ref, z_ref, lb_ref, nw_ref, s0_ref, a_ref, s_out_ref, *, n_tok):
    rows = _iota((SUB, HG_DV), 0)
    nw = nw_ref[...]
    rs = slice(0, SUB)
    for h in range(HG_HEADS):
        hs = slice(h * HG_DK, (h + 1) * HG_DK)
        fg, _, k = _hgrn_gates(f_ref[:, hs], lb_ref[:, hs])
        q = _silu(q_ref[:, hs].astype(F32))
        v = v_ref[:, hs].astype(F32)
        tile = jnp.concatenate([fg, k, q, jnp.zeros((LANES - 3 * SUB, HG_DK), F32)], axis=0)
        tile_t = tile.T
        S = s0_ref[h]
        o = jnp.zeros((SUB, HG_DV), F32)
        for t in range(n_tok):
            S = S * tile_t[:, t:t + 1] + tile_t[:, SUB + t:SUB + t + 1] * v[t:t + 1, :]
            o_t = jnp.sum(S * tile_t[:, 2 * SUB + t:2 * SUB + t + 1], axis=0, keepdims=True)
            o = jnp.where(rows == t, o_t, o)
        s_out_ref[h] = S
        _hgrn_out(o, nw, z_ref, rs, hs, a_ref)


def _hgrn_sample(ub, uf, lb, nw, s0, batch, n_tok):
    return pl.pallas_call(
        functools.partial(_hgrn_step_body, n_tok=n_tok),
        out_shape=(jax.ShapeDtypeStruct((batch * SUB, 512), BF16),
                   jax.ShapeDtypeStruct((batch, HG_HEADS, HG_DK, HG_DV), F32)),
        grid=(batch,),
        in_specs=[pl.BlockSpec((SUB, 512), lambda b: (b, UB_HGQ // 512)),
                  pl.BlockSpec((SUB, 512), lambda b: (b, UB_HGI // 512)),
                  pl.BlockSpec((SUB, 512), lambda b: (b, UF_HGF // 512)),
                  pl.BlockSpec((SUB, 512), lambda b: (b, UB_HGZ // 512)),
                  pl.BlockSpec((1, 512), lambda b: (0, 0)),
                  pl.BlockSpec((1, HG_DV), lambda b: (0, 0)),
                  pl.BlockSpec((None, HG_HEADS, HG_DK, HG_DV), lambda b: (b, 0, 0, 0))],
        out_specs=(pl.BlockSpec((SUB, 512), lambda b: (b, 0)),
                   pl.BlockSpec((None, HG_HEADS, HG_DK, HG_DV), lambda b: (b, 0, 0, 0))),
        compiler_params=_params("arbitrary"),
        name="hgrn_sample",
    )(ub, ub, uf, ub, lb, nw, s0)


def _softmax_update(carry, s, v_fn):
    m, l, acc = carry
    m_new = jnp.maximum(m, jnp.max(s, axis=-1, keepdims=True))
    alpha = jnp.exp(m - m_new)
    p = jnp.exp(s - m_new)
    return m_new, alpha * l + jnp.sum(p, axis=-1, keepdims=True), alpha * acc + v_fn(p.astype(BF16))


def _softmax_init(rows, width=LANES):
    return (jnp.full((rows, 1), NEG, F32), jnp.zeros((rows, 1), F32), jnp.zeros((rows, width), F32))


def _half_mask(x, lane, upper):
    keep = (lane >= 64) if upper else (lane < 64)
    return jnp.where(keep, x, jnp.zeros_like(x))


def _scaled_q(q, scale):
    return (q.astype(F32) * scale).astype(BF16)


def _diff_finish(o, lam, oml, nw, z, r_rows):
    od = o[:r_rows] - lam * o[r_rows:]
    ms = jnp.mean(od * od, axis=-1, keepdims=True)
    return (od * lax.rsqrt(ms + EPS) * nw * oml * _silu(z.astype(F32))).astype(BF16)


def _tflash_loop(lo, hi, score_fn, vt_fn, n, init=None):
    def body(kt, carry):
        m, l, acc = carry
        s, c_row = score_fn(kt)
        m_new = jnp.maximum(m, jnp.max(s, axis=0, keepdims=True) + c_row)
        p = jnp.exp2(s - (m_new - c_row))
        alpha = jnp.exp2(m - m_new)
        return m_new, alpha * l + jnp.sum(p, axis=0, keepdims=True), alpha * acc + _dot(vt_fn(kt), p.astype(BF16))

    if init is None:
        init = (jnp.full((1, n), NEG, F32), jnp.zeros((1, n), F32), jnp.zeros((LANES, n), F32))
    return lax.fori_loop(lo, hi, body, init)


def _fill_transposed(dst3, src_ref, cols, seq, tk):
    for t in range(seq // LANES):
        r0 = t * LANES
        dst3[r0 // tk, :, r0 % tk:r0 % tk + LANES] = src_ref[r0:r0 + LANES, cols].T.astype(BF16)


def _untranspose(x_t, tq):
    return jnp.concatenate([x_t[:, c:c + LANES].T for c in range(0, tq, LANES)], axis=0)


def _rel_t(tk, tq):
    return (_iota((tk, tq), 1) - _iota((tk, tq), 0)).astype(F32)


def _diff_prompt_body(sc_ref, q_ref, kv_ref, z_ref, nw_ref, a_ref, kb, vt, *, tq, tk, seq):
    g = pl.program_id(1)
    i = pl.program_id(2)
    R = DF_HEADS // DF_KV

    @pl.when(i == 0)
    def _():
        kb[...] = kv_ref[:, 0:128].astype(BF16)
        _fill_transposed(vt, kv_ref, slice(128, 256), seq, tk)

    lam, oml = sc_ref[0], sc_ref[1]
    lane = _iota((tq, LANES), 1)
    rel = _rel_t(tk, tq)
    qs, nrel, srow = [], [], []
    for r in range(R):
        slope = sc_ref[2 + g * R + r] * LOG2E
        q = _scaled_q(q_ref[:, r * 128:(r + 1) * 128], DF_HD ** -0.5 * LOG2E)
        qs += [_half_mask(q, lane, False), _half_mask(q, lane, True)]
        nrel += [-slope * rel] * 2
        srow += [jnp.full((1, tq), slope, F32)] * 2
    q4 = jnp.concatenate(qs, axis=0)
    nrel4 = jnp.concatenate(nrel, axis=1)
    srow4 = jnp.concatenate(srow, axis=1)

    def scores(kt, masked):
        off = pl.multiple_of(kt * tk, tk)
        d0 = (i * tq - kt * tk).astype(F32)
        s = _dot_nt(kb[pl.ds(off, tk), :], q4) + nrel4
        if masked:
            s = s + jnp.concatenate([jnp.where(rel + d0 >= 0, 0.0, NEG)] * (2 * R), axis=1)
        return s, -d0 * srow4

    n_full = (i * tq + 1) // tk
    n_t = ((i + 1) * tq + tk - 1) // tk
    carry = _tflash_loop(0, n_full, lambda kt: scores(kt, False), lambda kt: vt[kt], 2 * R * tq)
    _, l, acc = _tflash_loop(n_full, n_t, lambda kt: scores(kt, True), lambda kt: vt[kt], 2 * R * tq, carry)
    o_t = acc / l
    for r in range(R):
        od = _untranspose(o_t[:, 2 * r * tq:(2 * r + 1) * tq] - lam * o_t[:, (2 * r + 1) * tq:(2 * r + 2) * tq], tq)
        ms = jnp.mean(od * od, axis=-1, keepdims=True)
        cs = slice(r * 128, (r + 1) * 128)
        a_ref[:, cs] = (od * lax.rsqrt(ms + EPS) * nw_ref[...] * oml * _silu(z_ref[:, cs].astype(F32))).astype(BF16)


def _diff_prompt(sc, ub, ukv, nw, batch, seq):
    tq = min(512, seq)
    tk = min(256, seq)
    nq = seq // tq
    row = lambda b, g, i: b * nq + i
    return pl.pallas_call(
        functools.partial(_diff_prompt_body, tq=tq, tk=tk, seq=seq),
        out_shape=jax.ShapeDtypeStruct((batch * seq, 512), BF16),
        grid=(batch, DF_KV, nq),
        in_specs=[_smem(),
                  pl.BlockSpec((tq, 256), lambda b, g, i: (row(b, g, i), UB_DFQ // 256 + g)),
                  pl.BlockSpec((seq, 256), lambda b, g, i: (b, KV_DIFF // 256 + g)),
                  pl.BlockSpec((tq, 256), lambda b, g, i: (row(b, g, i), UB_DFZ // 256 + g)),
                  pl.BlockSpec((1, DF_VD), lambda b, g, i: (0, 0))],
        out_specs=pl.BlockSpec((tq, 256), lambda b, g, i: (row(b, g, i), g)),
        scratch_shapes=[pltpu.VMEM((seq, 128), BF16), pltpu.VMEM((seq // tk, 128, tk), BF16)],
        compiler_params=_params("arbitrary", "arbitrary", "arbitrary"),
        name="diff_prompt",
    )(sc, ub, ukv, ub, nw)


def _diff_sample_body(pt_ref, sc_ref, q_ref, new_ref, z_ref, nw_ref, *rest, PP, past, n_tok):
    del pt_ref
    pages = rest[:PP]
    a_ref = rest[PP]
    m_s, l_s, acc_s = rest[PP + 1:]
    c = pl.program_id(1)
    R = DF_HEADS // DF_KV
    rows = 2 * R * SUB

    @pl.when(c == 0)
    def _():
        m_s[...] = jnp.full(m_s.shape, NEG, F32)
        l_s[...] = jnp.zeros(l_s.shape, F32)
        acc_s[...] = jnp.zeros(acc_s.shape, F32)

    lane = _iota((SUB, LANES), 1)
    ridx = _iota((rows, 1), 0)
    tok = ridx & (SUB - 1)
    head = (ridx >> 3) & (R - 1)
    qpos = (past + tok).astype(F32)

    def make_q(g):
        pieces = []
        for m in range(2):
            for r in range(R):
                qs = _scaled_q(q_ref[:, g * 256 + r * 128:g * 256 + (r + 1) * 128], DF_HD ** -0.5)
                pieces.append(_half_mask(qs, lane, m == 1))
        slope = jnp.where(head == 0, sc_ref[2 + g * R], sc_ref[2 + g * R + 1])
        return jnp.concatenate(pieces, axis=0), slope

    def page_rows(p, g, j):
        return pages[p][pl.ds(2 * j + g, PAGE_SIZE, stride=2 * DF_KV), :].astype(BF16)

    for g in range(DF_KV):
        qg, slope = make_q(g)
        k_all = jnp.concatenate([page_rows(p, g, 0) for p in range(PP)], axis=0)
        v_all = jnp.concatenate([page_rows(p, g, 1) for p in range(PP)], axis=0)
        kpos = (c * (PP * PAGE_SIZE) + _iota((rows, PP * PAGE_SIZE), 1)).astype(F32)
        s = _dot_nt(qg, k_all) - slope * (qpos - kpos)
        m, l, acc = _softmax_update((m_s[g], l_s[g], acc_s[g]), s, lambda p: _dot(p, v_all))
        m_s[g], l_s[g], acc_s[g] = m, l, acc

    @pl.when(c == pl.num_programs(1) - 1)
    def _():
        lam, oml = sc_ref[0], sc_ref[1]
        pad = jnp.zeros((LANES - SUB, LANES), F32)
        jcol = _iota((rows, LANES), 1)
        for g in range(DF_KV):
            qg, slope = make_q(g)
            ks = g * 256
            knew = jnp.concatenate([new_ref[:, ks:ks + 128], pad], axis=0).astype(BF16)
            vnew = jnp.concatenate([new_ref[:, ks + 128:ks + 256], pad], axis=0).astype(BF16)
            dist = (tok - jcol).astype(F32)
            s = jnp.where((jcol <= tok) & (jcol < n_tok), _dot_nt(qg, knew) - slope * dist, NEG)
            _, l, acc = _softmax_update((m_s[g], l_s[g], acc_s[g]), s, lambda p: _dot(p, vnew))
            o = acc / l
            for r in range(R):
                o_r = jnp.concatenate([o[r * SUB:(r + 1) * SUB], o[(R + r) * SUB:(R + r + 1) * SUB]], axis=0)
                cs = slice(g * 256 + r * 128, g * 256 + (r + 1) * 128)
                a_ref[:, cs] = _diff_finish(o_r, lam, oml, nw_ref[...], z_ref[:, cs], SUB)


def _diff_sample(l, sc, pt_flat, ub, ukv, nw, cache, batch, n_pages, n_phys, n_tok):
    PP = min(16, n_pages)
    past = n_pages * PAGE_SIZE
    R = DF_HEADS // DF_KV
    rows = 2 * R * SUB

    def page_spec(p):
        return pl.BlockSpec((None, 4 * PAGE_SIZE, LANES),
                            lambda b, c, pt: (l * n_phys + pt[b * n_pages + c * PP + p], 0, 0))

    grid_spec = pltpu.PrefetchScalarGridSpec(
        num_scalar_prefetch=1,
        grid=(batch, n_pages // PP),
        in_specs=[_smem(),
                  pl.BlockSpec((SUB, 512), lambda b, c, pt: (b, UB_DFQ // 512)),
                  pl.BlockSpec((SUB, 512), lambda b, c, pt: (b, KV_DIFF // 512)),
                  pl.BlockSpec((SUB, 512), lambda b, c, pt: (b, UB_DFZ // 512)),
                  pl.BlockSpec((1, DF_VD), lambda b, c, pt: (0, 0))] + [page_spec(p) for p in range(PP)],
        out_specs=pl.BlockSpec((SUB, 512), lambda b, c, pt: (b, 0)),
        scratch_shapes=[pltpu.VMEM((DF_KV, rows, 1), F32), pltpu.VMEM((DF_KV, rows, 1), F32),
                        pltpu.VMEM((DF_KV, rows, DF_VD), F32)])
    return pl.pallas_call(
        functools.partial(_diff_sample_body, PP=PP, past=past, n_tok=n_tok),
        out_shape=jax.ShapeDtypeStruct((batch * SUB, 512), BF16),
        grid_spec=grid_spec,
        compiler_params=_params("arbitrary", "arbitrary"),
        name="diff_sample",
    )(pt_flat, sc, ub, ukv, ub, nw, *([cache] * PP))


def _block_sum(rows_ref, r0, cw):
    return jnp.sum(rows_ref[r0:r0 + CMP_BLK, 0:256] * cw, axis=0, keepdims=True)


def _top_blocks(imp, rowq, lane, n_blocks, n_pick):
    jq = rowq >> 6
    forced = (lane == 0) | (lane == jq) | (lane == jq - 1)
    score = jnp.where((lane <= jq) & (lane < n_blocks), imp + jnp.where(forced, FORCE, 0.0), NEG)
    lanef = lane.astype(F32)
    sel = jnp.zeros(imp.shape, F32)
    picks = []
    for _ in range(n_pick):
        mx = jnp.max(score, axis=-1, keepdims=True)
        am = jnp.min(jnp.where(score == mx, lanef, float(LANES)), axis=-1, keepdims=True)
        hit = lanef == am
        picked = mx > 0.5 * NEG
        sel = jnp.where(hit, jnp.where(picked, 1.0, 0.0), sel)
        score = jnp.where(hit, REMOVED, score)
        picks.append(jnp.where(picked, am, -1.0))
    return sel, picks


def _rank_select(score, n_blocks, n_pick):
    nv = score.shape[0] // SUB
    parts = [score[v * SUB:(v + 1) * SUB] for v in range(nv)]
    rank = [jnp.zeros(parts[0].shape, F32) for _ in range(nv)]
    rows = _iota(parts[0].shape, 0)
    for j in range(n_blocks):
        row = score[j:j + 1, :]
        for v in range(nv):
            if v * SUB > j:
                beats = jnp.where(row >= parts[v], 1.0, 0.0)
            elif v * SUB + SUB - 1 <= j:
                beats = jnp.where(row > parts[v], 1.0, 0.0)
            else:
                beats = jnp.where(rows + v * SUB > j, jnp.where(row >= parts[v], 1.0, 0.0),
                                  jnp.where(row > parts[v], 1.0, 0.0))
            rank[v] = rank[v] + beats
    return jnp.concatenate(
        [jnp.where((rank[v] < n_pick) & (parts[v] > 0.5 * NEG), 1.0, 0.0) for v in range(nv)], axis=0)


def _nsa_prompt_body(sl_ref, q_ref, gn_ref, z_ref, nsa_ref, win_ref, cw_ref, a_ref,
                     kce, kco, vce, vco, ksb, vst, kwb, vwt, *, tq, tk, seq):
    i = pl.program_id(1)
    nb = seq // SEL_BLK
    nbp = -(-nb // SUB) * SUB

    @pl.when(i == 0)
    def _():
        ksb[...] = nsa_ref[:, 256:384].astype(BF16)
        kwb[...] = win_ref[:, 0:128].astype(BF16)
        _fill_transposed(vst, nsa_ref, slice(384, 512), seq, tk)
        _fill_transposed(vwt, win_ref, slice(128, 256), seq, tk)
        kce[...] = jnp.zeros(kce.shape, F32)
        kco[...] = jnp.zeros(kco.shape, F32)
        cw = cw_ref[...]
        rows8 = _iota((SUB, 256), 0)
        v_even, v_odd = [], []
        for t8 in range(nbp // SUB):
            te = jnp.zeros((SUB, 256), F32)
            to = jnp.zeros((SUB, 256), F32)
            for e in range(SUB):
                j = t8 * SUB + e
                if j < nb:
                    te = jnp.where(rows8 == e, _block_sum(nsa_ref, 2 * j * CMP_BLK, cw), te)
                    to = jnp.where(rows8 == e, _block_sum(nsa_ref, (2 * j + 1) * CMP_BLK, cw), to)
            rs = slice(t8 * SUB, (t8 + 1) * SUB)
            kce[rs, :], kco[rs, :] = te[:, :128], to[:, :128]
            v_even.append(te[:, 128:])
            v_odd.append(to[:, 128:])
        pad = [jnp.zeros((LANES - nbp, LANES), F32)] if nbp < LANES else []
        vce[...] = jnp.concatenate(v_even + pad, axis=0).T
        vco[...] = jnp.concatenate(v_odd + pad, axis=0).T

    H = NS_KV * NS_R
    lane = _iota((tq, LANES), 1)
    blk = _iota((nbp, tq), 0)
    qpos = _iota((nbp, tq), 1) + i * tq
    qposf = qpos.astype(F32)
    kc = (kce[0:nbp, :].astype(BF16), kco[0:nbp, :].astype(BF16))
    vct = (vce[:, 0:nbp].astype(BF16), vco[:, 0:nbp].astype(BF16))
    cend = (blk * SEL_BLK + (CMP_BLK - 1), blk * SEL_BLK + (SEL_BLK - 1))
    valid = tuple((qpos >= ce) & (blk < nb) for ce in cend)
    qp = [_scaled_q(q_ref[:, r * 128:(r + 1) * 128], NS_HD ** -0.5 * LOG2E) for r in range(NS_R)]
    slopes = [sl_ref[h] * LOG2E for h in range(H)]
    qms, o_cmp, sel_t = [], [], []
    for g in range(NS_KV):
        imp = jnp.zeros((nbp, tq), F32)
        for r in range(NS_R):
            qm = _half_mask(qp[r], lane, g == 1)
            slope = slopes[g * NS_R + r]
            s = [jnp.where(valid[par], _dot_nt(kc[par], qm) - slope * (qposf - cend[par].astype(F32)), NEG)
                 for par in range(2)]
            mx = jnp.maximum(jnp.max(s[0], axis=0, keepdims=True), jnp.max(s[1], axis=0, keepdims=True))
            e = [jnp.where(valid[par], jnp.exp2(s[par] - mx), 0.0) for par in range(2)]
            den = jnp.maximum(jnp.sum(e[0], axis=0, keepdims=True) + jnp.sum(e[1], axis=0, keepdims=True), TINY)
            p = [e[0] / den, e[1] / den]
            qms.append(qm)
            o_cmp.append(_dot(vct[0], p[0].astype(BF16)) + _dot(vct[1], p[1].astype(BF16)))
            imp = imp + p[0] + p[1]
        jq = qpos >> 6
        forced = (blk == 0) | (blk == jq) | (blk == jq - 1)
        score = jnp.where((blk <= jq) & (blk < nb), imp + jnp.where(forced, FORCE, 0.0), NEG)
        sel_t.append(_rank_select(score, nb, SEL_N).astype(BF16))

    rel = _rel_t(tk, tq)
    nrel = [-slopes[h] * rel for h in range(H)]
    srow8 = jnp.concatenate([jnp.full((1, tq), slopes[h], F32) for h in range(H)], axis=1)
    q8 = jnp.concatenate(qms, axis=0)

    def scores(kt, k_ref, bias_fn):
        off = pl.multiple_of(kt * tk, tk)
        d0 = (i * tq - kt * tk).astype(F32)
        bias = bias_fn(off, rel + d0)
        s = _dot_nt(k_ref[pl.ds(off, tk), :], q8) + jnp.concatenate(
            [nrel[h] + bias[h // NS_R] for h in range(H)], axis=1)
        return s, -d0 * srow8

    def sel_bias(off, dist):
        expand = jnp.where(_iota((tk, nbp), 1) == ((_iota((tk, nbp), 0) + off) >> 6), 1.0, 0.0).astype(BF16)
        return [jnp.where((_dot(expand, st) > 0.5) & (dist >= 0), 0.0, NEG) for st in sel_t]

    def win_bias(off, dist):
        return [jnp.where((dist >= 0) & (dist <= WINDOW), 0.0, NEG)] * NS_KV

    n_t = ((i + 1) * tq + tk - 1) // tk
    _, l_sel, a_sel = _tflash_loop(0, n_t, lambda kt: scores(kt, ksb, sel_bias), lambda kt: vst[kt], H * tq)
    _, l_win, a_win = _tflash_loop(jnp.maximum(i * tq - WINDOW, 0) // tk, n_t,
                                   lambda kt: scores(kt, kwb, win_bias), lambda kt: vwt[kt], H * tq)
    o_sel, o_win = a_sel / l_sel, a_win / l_win
    gates = _sigmoid(gn_ref[...])
    gates_t = jnp.concatenate([gates[c:c + LANES, :].T for c in range(0, tq, LANES)], axis=1)
    half = _iota((LANES, tq), 0) < NS_HD
    for r in range(NS_R):
        og = []
        for g in range(NS_KV):
            h = g * NS_R + r
            cs = slice(h * tq, (h + 1) * tq)
            og.append(gates_t[3 * h:3 * h + 1, :] * o_cmp[h] + gates_t[3 * h + 1:3 * h + 2, :] * o_sel[:, cs]
                      + gates_t[3 * h + 2:3 * h + 3, :] * o_win[:, cs])
        o = _untranspose(jnp.where(half, og[0], og[1]), tq)
        cs = slice(r * 128, (r + 1) * 128)
        a_ref[:, cs] = (o * _silu(z_ref[:, cs].astype(F32))).astype(BF16)


def _nsa_prompt(sl, ub, uf, ukv, cw, batch, seq):
    tq = min(256, seq)
    tk = min(256, seq)
    nq = seq // tq
    row = lambda b, i: b * nq + i
    return pl.pallas_call(
        functools.partial(_nsa_prompt_body, tq=tq, tk=tk, seq=seq),
        out_shape=jax.ShapeDtypeStruct((batch * seq, 512), BF16),
        grid=(batch, nq),
        in_specs=[_smem(),
                  pl.BlockSpec((tq, 512), lambda b, i: (row(b, i), UB_NSQ // 512)),
                  pl.BlockSpec((tq, 128), lambda b, i: (row(b, i), UF_NSG // 128)),
                  pl.BlockSpec((tq, 512), lambda b, i: (row(b, i), UB_NSZ // 512)),
                  pl.BlockSpec((seq, 512), lambda b, i: (b, KV_NSA // 512)),
                  pl.BlockSpec((seq, 256), lambda b, i: (b, KV_WIN // 256)),
                  pl.BlockSpec((CMP_BLK, 256), lambda b, i: (0, 0))],
        out_specs=pl.BlockSpec((tq, 512), lambda b, i: (row(b, i), 0)),
        scratch_shapes=([pltpu.VMEM((LANES, 128), F32)] * 4
                        + [pltpu.VMEM((seq, 128), BF16), pltpu.VMEM((seq // tk, 128, tk), BF16)] * 2),
        compiler_params=_params("arbitrary", "arbitrary"),
        name="nsa_prompt",
    )(sl, ub, uf, ub, ukv, ukv, cw)


def _nsa_sample_a_body(pt_ref, sl_ref, q_ref, gn_ref, buf_ref, wnew_ref, wk_ref, wv_ref, *rest,
                       PP, past, n_tok, n_groups):
    del pt_ref
    pages = rest[:PP]
    part_ref, idx_ref = rest[PP:PP + 2]
    kct, vct = rest[PP + 2:]
    c = pl.program_id(1)
    nsb = past // SEL_BLK

    @pl.when(c == 0)
    def _():
        kct[...] = jnp.zeros(kct.shape, F32)
        vct[...] = jnp.zeros(vct.shape, F32)

    grp = (c * PP) // CMP_PAGES
    w0 = pl.multiple_of(((c * PP) % CMP_PAGES) * PAGE_SIZE, PP * PAGE_SIZE)
    k_all = jnp.concatenate([pages[p][0:128, :].astype(BF16) for p in range(PP)], axis=1)
    v_all = jnp.concatenate([pages[p][128:256, :].astype(BF16) for p in range(PP)], axis=1)
    kct[grp] = kct[grp] + _dot(k_all, wk_ref[pl.ds(w0, PP * PAGE_SIZE), :])
    vct[grp] = vct[grp] + _dot(v_all, wv_ref[pl.ds(w0, PP * PAGE_SIZE), :])

    @pl.when(c == pl.num_programs(1) - 1)
    def _():
        lane = _iota((SUB, LANES), 1)
        rowt = _iota((SUB, LANES), 0)
        rowq = rowt + past
        qposf = rowq.astype(F32)
        kc = [kct[G].astype(BF16) for G in range(n_groups)]
        vc = [vct[G].astype(BF16) for G in range(n_groups)]
        cend, valid = [], []
        for G in range(n_groups):
            blk = G * (LANES // 2) + (lane & (LANES // 2 - 1))
            ce = blk * SEL_BLK + (CMP_BLK - 1) + (lane >> 6) * CMP_BLK
            cend.append(ce)
            valid.append((rowq >= ce) & (blk < nsb))
        gates = _sigmoid(gn_ref[...])
        nbuf = buf_ref.shape[1]
        kw = buf_ref[0:128, :].astype(BF16)
        vw = buf_ref[128:256, :].astype(BF16)
        pad = jnp.zeros((LANES - SUB, LANES), F32)
        kwn = jnp.concatenate([wnew_ref[:, 0:128], pad], axis=0).astype(BF16)
        vwn = jnp.concatenate([wnew_ref[:, 128:256], pad], axis=0).astype(BF16)
        rows4 = NS_R * SUB
        tok4 = _iota((rows4, 1), 0) & (SUB - 1)
        head4 = _iota((rows4, 1), 0) >> 3
        outs = []
        for g in range(NS_KV):
            qms, o_cmp = [], []
            imps = [jnp.zeros((SUB, LANES), F32) for _ in range(n_groups)]
            for r in range(NS_R):
                qm = _half_mask(_scaled_q(q_ref[:, r * 128:(r + 1) * 128], NS_HD ** -0.5), lane, g == 1)
                slope = sl_ref[g * NS_R + r]
                s = [jnp.where(valid[G], _dot(qm, kc[G]) - slope * (qposf - cend[G].astype(F32)), NEG)
                     for G in range(n_groups)]
                mx = functools.reduce(jnp.maximum, [jnp.max(x, axis=-1, keepdims=True) for x in s])
                e = [jnp.where(valid[G], jnp.exp(s[G] - mx), 0.0) for G in range(n_groups)]
                den = jnp.maximum(sum(jnp.sum(x, axis=-1, keepdims=True) for x in e), TINY)
                o = jnp.zeros((SUB, LANES), F32)
                for G in range(n_groups):
                    p = e[G] / den
                    o = o + _dot_nt(p.astype(BF16), vc[G])
                    imps[G] = imps[G] + p
                qms.append(qm)
                o_cmp.append(o)
            imps = [x + pltpu.roll(x, LANES // 2, 1) for x in imps]
            imp = imps[0] if n_groups == 1 else jnp.where(lane < LANES // 2, imps[0], imps[1])
            _, picks = _top_blocks(imp, rowq, lane, nsb, SEL_N - 1)
            idx = jnp.full((SUB, LANES), -1.0, F32)
            for kk, pk in enumerate(picks):
                idx = jnp.where(lane == kk, pk, idx)
            idx_ref[g] = idx.astype(jnp.int32)
            q4 = jnp.concatenate(qms, axis=0)
            slope4 = jnp.zeros((rows4, 1), F32)
            for r in range(NS_R):
                slope4 = jnp.where(head4 == r, sl_ref[g * NS_R + r], slope4)
            dist_b = tok4 + (nbuf - _iota((rows4, nbuf), 1))
            s_b = jnp.where((dist_b >= 0) & (dist_b <= WINDOW),
                            _dot(q4, kw) - slope4 * dist_b.astype(F32), NEG)
            jn = _iota((rows4, LANES), 1)
            s_n = jnp.where((jn <= tok4) & (jn < n_tok), _dot_nt(q4, kwn) - slope4 * (tok4 - jn).astype(F32), NEG)
            mx = jnp.maximum(jnp.max(s_b, axis=-1, keepdims=True), jnp.max(s_n, axis=-1, keepdims=True))
            p_b, p_n = jnp.exp(s_b - mx), jnp.exp(s_n - mx)
            den = jnp.sum(p_b, axis=-1, keepdims=True) + jnp.sum(p_n, axis=-1, keepdims=True)
            o_win = (_dot_nt(p_b.astype(BF16), vw) + _dot(p_n.astype(BF16), vwn)) / den
            og = []
            for r in range(NS_R):
                col = (g * NS_R + r) * 3
                og.append(gates[:, col:col + 1] * o_cmp[r]
                          + gates[:, col + 2:col + 3] * o_win[r * SUB:(r + 1) * SUB])
            outs.append(og)
        for r in range(NS_R):
            part_ref[:, r * 128:(r + 1) * 128] = jnp.where(lane < 64, outs[0][r], outs[1][r])


def _nsa_sample_a(l, sl, pt_flat, ub, uf, ukv, buf_t, wk, wv, cache_t, batch, n_pages, n_phys, n_tok):
    PP = min(16, n_pages)
    past = n_pages * PAGE_SIZE
    nbuf = buf_t.shape[2]
    n_groups = -(-n_pages // CMP_PAGES)
    assert CMP_PAGES % PP == 0

    def page_spec(p):
        return pl.BlockSpec((None, 256, PAGE_SIZE),
                            lambda b, c, pt: (l * n_phys + pt[b * n_pages + c * PP + p], 0, 0))

    grid_spec = pltpu.PrefetchScalarGridSpec(
        num_scalar_prefetch=1,
        grid=(batch, n_pages // PP),
        in_specs=[_smem(),
                  pl.BlockSpec((SUB, 512), lambda b, c, pt: (b, UB_NSQ // 512)),
                  pl.BlockSpec((SUB, 128), lambda b, c, pt: (b, UF_NSG // 128)),
                  pl.BlockSpec((None, 256, nbuf), lambda b, c, pt: (l * batch + b, 0, 0)),
                  pl.BlockSpec((SUB, 256), lambda b, c, pt: (b, KV_WIN // 256)),
                  pl.BlockSpec((CMP_PAGES * PAGE_SIZE, LANES), lambda b, c, pt: (0, 0)),
                  pl.BlockSpec((CMP_PAGES * PAGE_SIZE, LANES), lambda b, c, pt: (0, 0))]
                 + [page_spec(p) for p in range(PP)],
        out_specs=(pl.BlockSpec((SUB, 512), lambda b, c, pt: (b, 0)),
                   pl.BlockSpec((None, NS_KV, SUB, LANES), lambda b, c, pt: (b, 0, 0, 0))),
        scratch_shapes=[pltpu.VMEM((n_groups, LANES, LANES), F32)] * 2)
    return pl.pallas_call(
        functools.partial(_nsa_sample_a_body, PP=PP, past=past, n_tok=n_tok, n_groups=n_groups),
        out_shape=(jax.ShapeDtypeStruct((batch * SUB, 512), F32),
                   jax.ShapeDtypeStruct((batch, NS_KV, SUB, LANES), jnp.int32)),
        grid_spec=grid_spec,
        compiler_params=_params("arbitrary", "arbitrary"),
        name="nsa_sample_a",
    )(pt_flat, sl, ub, uf, buf_t, ukv, wk, wv, *([cache_t] * PP))


def _nsa_sample_b_body(pt_ref, idx_ref, sl_ref, q_ref, gn_ref, z_ref, new_ref, part_ref, *rest, NSEL, past, n_tok):
    del pt_ref
    blocks = rest[:NSEL]
    a_ref = rest[NSEL]
    acc_scr = rest[NSEL + 1]
    b, g, t = pl.program_id(0), pl.program_id(1), pl.program_id(2)

    @pl.when((g == 0) & (t == 0))
    def _():
        acc_scr[...] = part_ref[...]

    lane = _iota((SUB, LANES), 1)
    rows = _iota((SUB, LANES), 0)
    ghalf = (lane >> 6) == g
    rcol = _iota((SUB, 1), 0)
    gates = _sigmoid(gn_ref[...])
    q = jnp.zeros((SUB, LANES), F32)
    slope = jnp.zeros((SUB, 1), F32)
    gate = jnp.zeros((SUB, 1), F32)
    for r in range(NS_R):
        qp = q_ref[:, r * 128:(r + 1) * 128].astype(F32)
        q = jnp.where(rows == r, jnp.sum(jnp.where(rows == t, qp, 0.0), axis=0, keepdims=True), q)
        slope = jnp.where(rcol == r, sl_ref[g * NS_R + r], slope)
        gv = jnp.where((rows == t) & (lane == (g * NS_R + r) * 3 + 1), gates, 0.0)
        gv = jnp.sum(jnp.sum(gv, axis=1, keepdims=True), axis=0, keepdims=True)
        gate = jnp.where(rcol == r, gv, gate)
    qb = jnp.where(ghalf, q * NS_HD ** -0.5, 0.0).astype(BF16)
    qpos = past + t
    base = ((b * NS_KV + g) * n_tok + t) * SEL_N
    s_all = []
    for k in range(NSEL):
        blk = idx_ref[base + k]
        half = jnp.where(blk >= 0, blk & 1, 2)
        kpos = (blk >> 1) * PAGE_SIZE + lane
        s = _dot(qb, blocks[k][0:128, :].astype(BF16)) - slope * (qpos - kpos).astype(F32)
        s_all.append(jnp.where((lane >> 6) == half, s, NEG))
    pad = jnp.zeros((LANES - SUB, LANES), F32)
    knew = jnp.concatenate([new_ref[:, 0:128], pad], axis=0).astype(BF16)
    vnew = jnp.concatenate([new_ref[:, 128:256], pad], axis=0).astype(BF16)
    s = _dot_nt(qb, knew) - slope * (t - lane).astype(F32)
    s_all.append(jnp.where((lane <= t) & (lane < n_tok), s, NEG))
    mx = functools.reduce(jnp.maximum, [x.max(axis=-1, keepdims=True) for x in s_all])
    den = jnp.zeros((SUB, 1), F32)
    o = jnp.zeros((SUB, LANES), F32)
    for k, s in enumerate(s_all):
        p = jnp.exp(s - mx)
        den = den + jnp.sum(p, axis=-1, keepdims=True)
        pb = p.astype(BF16)
        o = o + (_dot_nt(pb, blocks[k][128:256, :].astype(BF16)) if k < NSEL else _dot(pb, vnew))
    o = gate * o / den
    for r in range(NS_R):
        cs = slice(r * 128, (r + 1) * 128)
        acc_scr[:, cs] = acc_scr[:, cs] + jnp.where((rows == t) & ghalf, o[r:r + 1, :], 0.0)

    @pl.when((g == NS_KV - 1) & (t == n_tok - 1))
    def _():
        a_ref[...] = (acc_scr[...] * _silu(z_ref[...].astype(F32))).astype(BF16)


def _nsa_sample_b(l, sl, pt_flat, idx_flat, ub, uf, ukv, part, cache_t, batch, n_pages, n_phys, n_tok):
    NSEL = SEL_N - 1
    past = n_pages * PAGE_SIZE
    halves = PAGE_SIZE // SEL_BLK

    def blk_spec(k):
        def index(b, g, t, pt, idx):
            blk = jnp.maximum(idx[((b * NS_KV + g) * n_tok + t) * SEL_N + k], 0)
            return (l * n_phys + pt[b * n_pages + blk // halves], 1, 0)
        return pl.BlockSpec((None, 256, PAGE_SIZE), index)

    grid_spec = pltpu.PrefetchScalarGridSpec(
        num_scalar_prefetch=2,
        grid=(batch, NS_KV, n_tok),
        in_specs=[_smem(),
                  pl.BlockSpec((SUB, 512), lambda b, g, t, pt, idx: (b, UB_NSQ // 512)),
                  pl.BlockSpec((SUB, 128), lambda b, g, t, pt, idx: (b, UF_NSG // 128)),
                  pl.BlockSpec((SUB, 512), lambda b, g, t, pt, idx: (b, UB_NSZ // 512)),
                  pl.BlockSpec((SUB, 256), lambda b, g, t, pt, idx: (b, (KV_NSA + 256) // 256)),
                  pl.BlockSpec((SUB, 512), lambda b, g, t, pt, idx: (b, 0))] + [blk_spec(k) for k in range(NSEL)],
        out_specs=pl.BlockSpec((SUB, 512), lambda b, g, t, pt, idx: (b, 0)),
        scratch_shapes=[pltpu.VMEM((SUB, 512), F32)])
    return pl.pallas_call(
        functools.partial(_nsa_sample_b_body, NSEL=NSEL, past=past, n_tok=n_tok),
        out_shape=jax.ShapeDtypeStruct((batch * SUB, 512), BF16),
        grid_spec=grid_spec,
        compiler_params=_params("arbitrary", "arbitrary", "arbitrary"),
        name="nsa_sample_b",
    )(pt_flat, idx_flat, sl, ub, uf, ub, ukv, part, *([cache_t] * NSEL))


def _final_body(x_ref, gate_ref, gp_ref, ahg_ref, adf_ref, ans_ref, mg_ref, whg_ref, wdf_ref, wns_ref, wo_ref, o_ref):
    d = x_ref.shape[-1]
    merged = (_sigmoid(mg_ref[:, 0:d].astype(F32)) * _dot(ahg_ref[...], whg_ref[...])
              + _sigmoid(mg_ref[:, d:2 * d].astype(F32)) * _dot(adf_ref[...], wdf_ref[...])
              + _sigmoid(mg_ref[:, 2 * d:3 * d].astype(F32)) * _dot(ans_ref[...], wns_ref[...]))
    out = _dot(merged.astype(BF16), wo_ref[...])
    ms = jnp.mean(out * out, axis=-1, keepdims=True)
    o_ref[...] = x_ref[...] + gate_ref[...] * (out * lax.rsqrt(ms + EPS) * gp_ref[...])


def _final(x2, cond, l, g_post, a_hg, a_df, a_ns, ub, w_hg, w_df, w_ns, w_o, tm, rows_per_batch, name):
    rows, d = x2.shape
    wspec = lambda k: pl.BlockSpec((None, k, d), lambda i: (l, 0, 0))
    return pl.pallas_call(
        _final_body,
        out_shape=jax.ShapeDtypeStruct((rows, d), F32),
        grid=(rows // tm,),
        in_specs=[pl.BlockSpec((tm, d), lambda i: (i, 0)),
                  _cond_spec(cond, l, 2, tm, rows_per_batch),
                  pl.BlockSpec((None, 1, d), lambda i: (l, 0, 0)),
                  pl.BlockSpec((tm, 512), lambda i: (i, 0)),
                  pl.BlockSpec((tm, 512), lambda i: (i, 0)),
                  pl.BlockSpec((tm, 512), lambda i: (i, 0)),
                  pl.BlockSpec((tm, 3 * d), lambda i: (i, UB_MG // (3 * d))),
                  wspec(512), wspec(512), wspec(512), wspec(d)],
        out_specs=pl.BlockSpec((tm, d), lambda i: (i, 0)),
        compiler_params=_params("arbitrary"),
        name=name,
    )(x2, cond, g_post, a_hg, a_df, a_ns, ub, w_hg, w_df, w_ns, w_o)


def _alibi(n):
    return [2.0 ** (-8.0 * (h + 1) / n) for h in range(n)]


def kernel(x_prompt, x_sample, cache_diff, cache_nsa, cache_nsa_win, state_hgrn, page_table, c_prompt, c_sample,
           w_cond, b_cond, g_pre, g_post, w_in, hg_lb, hg_norm, df_lam, df_norm, ns_cmp,
           w_hg_out, w_df_out, w_ns_out, w_out):
    depth, d, _ = w_in.shape
    bp, seq, _ = x_prompt.shape
    bs, n_tok, _ = x_sample.shape
    n_pages = page_table.shape[1]
    n_phys = cache_diff.shape[1]
    past = n_pages * PAGE_SIZE
    assert d == 1024 and seq % 128 == 0 and n_tok <= SUB and n_pages % 4 == 0
    assert cache_nsa_win.shape[2] == WINDOW and past >= WINDOW and past // SEL_BLK <= LANES

    kv_cols, uf_cols, ub_cols, pair = _column_layout()
    w_kv = jnp.take(w_in, kv_cols, axis=2).astype(BF16)
    w_uf = jnp.where(uf_cols >= 0, jnp.take(w_in, np.maximum(uf_cols, 0), axis=2), 0.0).astype(BF16)
    w_ub = jnp.take(w_in, ub_cols, axis=2).astype(BF16)
    w_hg = w_hg_out.astype(BF16)
    w_df = w_df_out.astype(BF16)
    w_ns = jnp.take(w_ns_out, pair, axis=1).astype(BF16)
    w_o = w_out.astype(BF16)
    g_pre3 = g_pre.reshape(depth, 1, d)
    g_post3 = g_post.reshape(depth, 1, d)
    lb_w = jax.nn.softmax(hg_lb.astype(F32), axis=0)
    lower = jnp.cumsum(lb_w, axis=0) - lb_w[0]
    cw_all = jax.nn.softmax(ns_cmp.astype(F32), axis=-1)
    cw_tile = jnp.repeat(jnp.swapaxes(cw_all, 1, 2), 128, axis=2)
    lv = df_lam.astype(F32)
    lam_init = jnp.asarray([0.8 - 0.6 * math.exp(-0.3 * l) for l in range(depth)], F32)
    lam = jnp.exp(jnp.sum(lv[:, 0] * lv[:, 1], axis=-1)) - jnp.exp(jnp.sum(lv[:, 2] * lv[:, 3], axis=-1)) + lam_init
    df_sc = jnp.concatenate([lam[:, None], 1.0 - lam_init[:, None],
                             jnp.broadcast_to(jnp.asarray(_alibi(DF_HEADS), F32), (depth, DF_HEADS))], axis=1)
    ns_sl = jnp.asarray(_alibi(NS_HEADS), F32)

    bc = bp + bs
    bc_pad = -(-bc // SUB) * SUB
    c_all = jnp.concatenate([c_prompt, c_sample, jnp.zeros((bc_pad - bc, d), F32)], axis=0)
    cond = _cond_all(c_all, w_cond, b_cond)
    cond_p = cond[:, :bp].reshape(depth, bp, 1, 3 * d)
    cond_s = jnp.repeat(cond[:, bp:bc], SUB, axis=1)

    pt_flat = page_table.reshape(-1).astype(jnp.int32)
    cache_diff2 = cache_diff.reshape(depth, n_phys, PAGE_SIZE, DF_KV, 2, LANES).transpose(0, 1, 2, 4, 3, 5)
    cache_diff2 = cache_diff2.reshape(depth * n_phys, 4 * PAGE_SIZE, LANES)
    cache_nsa_t = cache_nsa.transpose(0, 1, 3, 4, 5, 2).reshape(depth * n_phys, 512, PAGE_SIZE)
    win_buf_t = cache_nsa_win.transpose(0, 1, 3, 4, 5, 2).reshape(depth * bs, 256, WINDOW)
    pos = np.arange(PAGE_SIZE)
    nn = pos // CMP_BLK
    tgt = (nn % 2)[None, :] * (LANES // 2) + 2 * np.arange(CMP_PAGES)[:, None] + (nn // 2)[None, :]
    place = jnp.asarray(tgt[:, :, None] == np.arange(LANES)[None, None, :], F32)
    cw_pos = cw_all[:, :, pos % CMP_BLK]
    w_cmp = (place[None, None] * cw_pos[:, :, None, :, None]).astype(BF16)
    w_cmp = w_cmp.reshape(depth, 2, CMP_PAGES * PAGE_SIZE, LANES)

    xp = x_prompt.reshape(bp * seq, d)
    xs = jnp.pad(x_sample, ((0, 0), (0, SUB - n_tok), (0, 0))).reshape(bs * SUB, d)
    zeros_state = jnp.zeros((bp, HG_HEADS, HG_DK, HG_DV), F32)
    tm_p = min(1024, seq)
    tm_f = min(512, seq)
    rows_s = bs * SUB

    outs = {k: [] for k in ('dp', 'ds', 'np', 'ns', 'wp', 'ws', 'hp', 'hs')}
    for l in range(depth):
        lb = lower[l].reshape(1, 512)
        nw_hg = hg_norm[l].reshape(1, HG_DV)
        nw_df = df_norm[l].reshape(1, DF_VD)

        ukv = _inproj(xp, cond_p, l, g_pre3, w_kv, KV_W, F32, tm_p, seq, "inproj_kv_p")
        uf = _inproj(xp, cond_p, l, g_pre3, w_uf, UF_W, F32, tm_p, seq, "inproj_f_p")
        ub = _inproj(xp, cond_p, l, g_pre3, w_ub, 512, BF16, tm_p, seq, "inproj_b_p")
        a_hg, s_p = _hgrn_prompt(ub, uf, lb, nw_hg, zeros_state, bp, seq)
        a_df = _diff_prompt(df_sc[l], ub, ukv, nw_df, bp, seq)
        a_ns = _nsa_prompt(ns_sl, ub, uf, ukv, cw_tile[l], bp, seq)
        xp = _final(xp, cond_p, l, g_post3, a_hg, a_df, a_ns, ub, w_hg, w_df, w_ns, w_o, tm_f, seq, "final_p")
        ukv3 = ukv.reshape(bp, seq, KV_W)
        outs['dp'].append(ukv3[:, :, KV_DIFF:KV_NSA].reshape(bp, seq, DF_KV, 256))
        outs['np'].append(ukv3[:, :, KV_NSA:KV_WIN].reshape(bp, seq, 4, NS_KV, NS_HD))
        outs['wp'].append(ukv3[:, seq - min(WINDOW, seq):, KV_WIN:].reshape(bp, min(WINDOW, seq), 2, NS_KV, NS_HD))
        outs['hp'].append(s_p)

        ukv = _inproj(xs, cond_s, l, g_pre3, w_kv, KV_W, F32, rows_s, SUB, "inproj_kv_s")
        uf = _inproj(xs, cond_s, l, g_pre3, w_uf, UF_W, F32, rows_s, SUB, "inproj_f_s")
        ub = _inproj(xs, cond_s, l, g_pre3, w_ub, 512, BF16, rows_s, SUB, "inproj_b_s")
        a_hg, s_s = _hgrn_sample(ub, uf, lb, nw_hg, state_hgrn[l], bs, n_tok)
        a_df = _diff_sample(l, df_sc[l], pt_flat, ub, ukv, nw_df, cache_diff2, bs, n_pages, n_phys, n_tok)
        part, idx = _nsa_sample_a(l, ns_sl, pt_flat, ub, uf, ukv, win_buf_t, w_cmp[l, 0], w_cmp[l, 1], cache_nsa_t,
                                  bs, n_pages, n_phys, n_tok)
        idx_flat = idx[:, :, :n_tok, :SEL_N].reshape(-1)
        a_ns = _nsa_sample_b(l, ns_sl, pt_flat, idx_flat, ub, uf, ukv, part, cache_nsa_t,
                             bs, n_pages, n_phys, n_tok)
        xs = _final(xs, cond_s, l, g_post3, a_hg, a_df, a_ns, ub, w_hg, w_df, w_ns, w_o, rows_s, SUB, "final_s")
        ukv3 = ukv.reshape(bs, SUB, KV_W)[:, :n_tok]
        outs['ds'].append(ukv3[:, :, KV_DIFF:KV_NSA].reshape(bs, n_tok, DF_KV, 256))
        outs['ns'].append(ukv3[:, :, KV_NSA:KV_WIN].reshape(bs, n_tok, 4, NS_KV, NS_HD))
        new_w = ukv3[:, :, KV_WIN:].reshape(bs, n_tok, 2, NS_KV, NS_HD)
        outs['ws'].append(jnp.concatenate([cache_nsa_win[l][:, n_tok:], new_w], axis=1))
        outs['hs'].append(s_s)

    st = {k: jnp.stack(v) for k, v in outs.items()}
    y_p = xp.reshape(bp, seq, d)
    y_s = xs.reshape(bs, SUB, d)[:, :n_tok]
    return (y_p, y_s, st['dp'], st['ds'], st['np'], st['ns'], st['wp'], st['ws'], st['hp'], st['hs'])
```

```python
import functools
import math

import numpy as np
import jax
import jax.numpy as jnp
from jax import lax
from jax.experimental import pallas as pl
from jax.experimental.pallas import tpu as pltpu

F32 = jnp.float32
BF16 = jnp.bfloat16

PAGE_SIZE = 128
HG_HEADS, HG_DK, HG_DV = 4, 128, 128
HG_CHUNK = 64
DF_HEADS, DF_KV, DF_HD = 4, 2, 64
DF_VD = 2 * DF_HD
NS_HEADS, NS_KV, NS_HD = 8, 2, 64
NS_R = NS_HEADS // NS_KV
CMP_BLK, SEL_BLK, SEL_N, WINDOW = 32, 64, 8, 512
EPS, NEG, TINY, FORCE = 1e-6, -1e30, 1e-30, 1e4
REMOVED = -3e38
EXP_CLAMP = 80.0
LOG2E = math.log2(math.e)
LANES = 128
SUB = 8
CMP_PAGES = 32
VMEM_LIMIT = 48 * 1024 * 1024

IN_SPLITS = (
    ('hg_q', 512), ('hg_f', 512), ('hg_i', 512), ('hg_z', 512),
    ('df_q', 512), ('df_k', 256), ('df_v', 256), ('df_z', 512),
    ('ns_q', 512), ('ns_kvc', 256), ('ns_kvs', 256), ('ns_kvw', 256), ('ns_g', 24), ('ns_z', 512),
    ('merge', 3072),
)

KV_DIFF, KV_NSA, KV_WIN, KV_W = 0, 512, 1024, 1280
UF_HGF, UF_NSG, UF_W = 0, 512, 640
UB_MG, UB_HGQ, UB_HGI, UB_DFQ, UB_NSQ, UB_HGZ, UB_DFZ, UB_NSZ, UB_W = (
    0, 3072, 3584, 4096, 4608, 5120, 5632, 6144, 6656)


def _column_layout():
    off, o = {}, 0
    for n, w in IN_SPLITS:
        off[n] = o
        o += w
    rng = lambda n, a, b: list(range(off[n] + a, off[n] + b))
    kv = []
    for g in range(DF_KV):
        kv += rng('df_k', g * 128, (g + 1) * 128) + rng('df_v', g * 128, (g + 1) * 128)
    kv += rng('ns_kvc', 0, 256) + rng('ns_kvs', 0, 256) + rng('ns_kvw', 0, 256)
    uf = rng('hg_f', 0, 512) + rng('ns_g', 0, 24) + [-1] * 104
    pair = [g * 256 + r * 64 + d for r in range(NS_R) for g in range(NS_KV) for d in range(NS_HD)]
    ub = (rng('merge', 0, 3072) + rng('hg_q', 0, 512) + rng('hg_i', 0, 512) + rng('df_q', 0, 512)
          + [off['ns_q'] + p for p in pair] + rng('hg_z', 0, 512) + rng('df_z', 0, 512)
          + [off['ns_z'] + p for p in pair])
    assert len(kv) == KV_W and len(uf) == UF_W and len(ub) == UB_W
    return np.asarray(kv), np.asarray(uf), np.asarray(ub), np.asarray(pair)


def _dot(a, b):
    return jnp.dot(a, b, preferred_element_type=F32)


def _dot_nt(a, b):
    return lax.dot_general(a, b, (((1,), (1,)), ((), ())), preferred_element_type=F32)


def _sigmoid(x):
    return 1.0 / (1.0 + jnp.exp(-x))


def _silu(x):
    return x * _sigmoid(x)


def _iota(shape, dim):
    return lax.broadcasted_iota(jnp.int32, shape, dim)


def _params(*sem):
    return pltpu.CompilerParams(dimension_semantics=sem, vmem_limit_bytes=VMEM_LIMIT)


def _smem():
    return pl.BlockSpec(memory_space=pltpu.SMEM)


def _cond_body(c_ref, w_ref, b_ref, o_ref):
    a = _silu(c_ref[...])
    a_hi = a.astype(BF16)
    a_lo = (a - a_hi.astype(F32)).astype(BF16)
    w = w_ref[...]
    w_hi = w.astype(BF16)
    w_lo = (w - w_hi.astype(F32)).astype(BF16)
    o_ref[...] = _dot(a_hi, w_hi) + (_dot(a_hi, w_lo) + _dot(a_lo, w_hi)) + b_ref[...]


def _cond_all(c_all, w_cond, b_cond):
    depth, d, _ = w_cond.shape
    bc = c_all.shape[0]
    return pl.pallas_call(
        _cond_body,
        out_shape=jax.ShapeDtypeStruct((depth, bc, 3 * d), F32),
        grid=(depth, 3),
        in_specs=[pl.BlockSpec((bc, d), lambda l, k: (0, 0)),
                  pl.BlockSpec((None, d, d), lambda l, k: (l, 0, k)),
                  pl.BlockSpec((None, 1, d), lambda l, k: (l, 0, k))],
        out_specs=pl.BlockSpec((None, bc, d), lambda l, k: (l, 0, k)),
        compiler_params=_params("arbitrary", "arbitrary"),
        name="cond",
    )(c_all, w_cond, b_cond.reshape(depth, 1, 3 * d))


def _inproj_body(x_ref, sh_ref, sc_ref, gp_ref, w_ref, o_ref, h_scr):
    @pl.when(pl.program_id(1) == 0)
    def _():
        x = x_ref[...]
        ms = jnp.mean(x * x, axis=-1, keepdims=True)
        y = x * lax.rsqrt(ms + EPS) * gp_ref[...]
        h_scr[...] = (y * (1.0 + sc_ref[...]) + sh_ref[...]).astype(BF16)

    o_ref[...] = _dot(h_scr[...], w_ref[...]).astype(o_ref.dtype)


def _cond_spec(cond, l, k, tm, rows_per_batch):
    d = cond.shape[-1] // 3
    if cond.ndim == 4:
        tiles = rows_per_batch // tm
        return pl.BlockSpec((None, None, 1, d), lambda i, *_: (l, i // tiles, 0, k))
    return pl.BlockSpec((None, tm, d), lambda i, *_: (l, i, k))


def _inproj(x2, cond, l, g_pre, w, tn, out_dtype, tm, rows_per_batch, name):
    rows, d = x2.shape
    n = w.shape[-1]
    return pl.pallas_call(
        _inproj_body,
        out_shape=jax.ShapeDtypeStruct((rows, n), out_dtype),
        grid=(rows // tm, n // tn),
        in_specs=[pl.BlockSpec((tm, d), lambda i, j: (i, 0)),
                  _cond_spec(cond, l, 0, tm, rows_per_batch),
                  _cond_spec(cond, l, 1, tm, rows_per_batch),
                  pl.BlockSpec((None, 1, d), lambda i, j: (l, 0, 0)),
                  pl.BlockSpec((None, d, tn), lambda i, j: (l, 0, j))],
        out_specs=pl.BlockSpec((tm, tn), lambda i, j: (i, j)),
        scratch_shapes=[pltpu.VMEM((tm, d), BF16)],
        compiler_params=_params("arbitrary", "arbitrary"),
        name=name,
    )(x2, cond, cond, g_pre, w)


def _hgrn_gates(fr, lbh):
    e = jnp.exp(-jnp.abs(fr))
    inv = 1.0 / (1.0 + e)
    pos = fr >= 0
    sg = jnp.where(pos, inv, e * inv)
    sgn = jnp.where(pos, e * inv, inv)
    fg = lbh + (1.0 - lbh) * sg
    return fg, jnp.log(jnp.maximum(fg, TINY)), (1.0 - lbh) * sgn


def _hgrn_out(o, nw, z_ref, rs, hs, a_ref):
    ms = jnp.mean(o * o, axis=-1, keepdims=True)
    on = o * lax.rsqrt(ms + EPS) * nw
    a_ref[rs, hs] = (on * _silu(z_ref[rs, hs].astype(F32))).astype(BF16)


def _hgrn_chunk_body(q_ref, v_ref, f_ref, z_ref, lb_ref, nw_ref, s0_ref, a_ref, s_out_ref, s_scr, *, C, n_chunks):
    t = pl.program_id(1)

    @pl.when(t == 0)
    def _():
        s_scr[...] = s0_ref[...]

    ts = C * n_chunks
    shift = C.bit_length() - 1
    ri, ci = _iota((ts, ts), 0), _iota((ts, ts), 1)
    tri = (ri >= ci) & ((ri >> shift) == (ci >> shift))
    tri_b = jnp.where(tri, 1.0, 0.0).astype(BF16)
    mid = C // 2
    nw = nw_ref[...]
    _, g, k = _hgrn_gates(f_ref[...], lb_ref[...])
    q = _silu(q_ref[...].astype(F32))
    g1 = g.astype(BF16)
    r1 = g - g1.astype(F32)
    g2 = r1.astype(BF16)
    g3 = (r1 - g2.astype(F32)).astype(BF16)
    G = _dot(tri_b, g1) + _dot(tri_b, g2) + _dot(tri_b, g3)

    def per_chunk_row(idx):
        return jnp.concatenate(
            [jnp.broadcast_to(G[c * C + idx:c * C + idx + 1, :], (C, G.shape[1])) for c in range(n_chunks)], axis=0)

    g_mid, g_last = per_chunk_row(mid - 1), per_chunk_row(C - 1)
    qe = (q * jnp.exp(jnp.minimum(G - g_mid, EXP_CLAMP))).astype(BF16)
    ke = (k * jnp.exp(jnp.minimum(g_mid - G, EXP_CLAMP))).astype(BF16)
    qg = (q * jnp.exp(G)).astype(BF16)
    kd = k * jnp.exp(g_last - G)
    pad = jnp.zeros((LANES - C - SUB, HG_DK), F32)
    for h in range(HG_HEADS):
        hs = slice(h * HG_DK, (h + 1) * HG_DK)
        att = jnp.where(tri, _dot_nt(qe[:, hs], ke[:, hs]), 0.0)
        o_intra = _dot(att.astype(BF16), v_ref[:, hs])
        S = s_scr[h]
        o_inter = []
        for c in range(n_chunks):
            rs = slice(c * C, (c + 1) * C)
            o_inter.append(_dot(qg[rs, hs], S.astype(BF16)))
            decay = jnp.exp(G[(c + 1) * C - 1:(c + 1) * C, hs])
            tile_t = jnp.concatenate([kd[rs, hs], jnp.broadcast_to(decay, (SUB, HG_DK)), pad], axis=0).T
            S = tile_t[:, C:C + 1] * S + _dot(tile_t[:, :C].astype(BF16), v_ref[rs, hs])
        s_scr[h] = S
        _hgrn_out(o_intra + jnp.concatenate(o_inter, axis=0), nw, z_ref, slice(0, ts), hs, a_ref)

    @pl.when(t == pl.num_programs(1) - 1)
    def _():
        s_out_ref[...] = s_scr[...]


def _hgrn_prompt(ub, uf, lb, nw, s0, batch, seq):
    ts = min(256, seq)
    C = math.gcd(ts, HG_CHUNK)
    nt = seq // ts
    row = lambda b, t: b * nt + t
    return pl.pallas_call(
        functools.partial(_hgrn_chunk_body, C=C, n_chunks=ts // C),
        out_shape=(jax.ShapeDtypeStruct((batch * seq, 512), BF16),
                   jax.ShapeDtypeStruct((batch, HG_HEADS, HG_DK, HG_DV), F32)),
        grid=(batch, nt),
        in_specs=[pl.BlockSpec((ts, 512), lambda b, t: (row(b, t), UB_HGQ // 512)),
                  pl.BlockSpec((ts, 512), lambda b, t: (row(b, t), UB_HGI // 512)),
                  pl.BlockSpec((ts, 512), lambda b, t: (row(b, t), UF_HGF // 512)),
                  pl.BlockSpec((ts, 512), lambda b, t: (row(b, t), UB_HGZ // 512)),
                  pl.BlockSpec((1, 512), lambda b, t: (0, 0)),
                  pl.BlockSpec((1, HG_DV), lambda b, t: (0, 0)),
                  pl.BlockSpec((None, HG_HEADS, HG_DK, HG_DV), lambda b, t: (b, 0, 0, 0))],
        out_specs=(pl.BlockSpec((ts, 512), lambda b, t: (row(b, t), 0)),
                   pl.BlockSpec((None, HG_HEADS, HG_DK, HG_DV), lambda b, t: (b, 0, 0, 0))),
        scratch_shapes=[pltpu.VMEM((HG_HEADS, HG_DK, HG_DV), F32)],
        compiler_params=_params("arbitrary", "arbitrary"),
        name="hgrn_prompt",
    )(ub, ub, uf, ub, lb, nw, s0)


def _hgrn_step_body(q_ref, v_ref, f_ref, z_ref, lb_ref, nw_ref, s0_ref, a_ref, s_out_ref, *, n_tok):
    rows = _iota((SUB, HG_DV), 0)
    nw = nw_ref[...]
    rs = slice(0, SUB)
    for h in range(HG_HEADS):
        hs = slice(h * HG_DK, (h + 1) * HG_DK)
        fg, _, k = _hgrn_gates(f_ref[:, hs], lb_ref[:, hs])
        q = _silu(q_ref[:, hs].astype(F32))
        v = v_ref[:, hs].astype(F32)
        tile = jnp.concatenate([fg, k, q, jnp.zeros((LANES - 3 * SUB, HG_DK), F32)], axis=0)
        tile_t = tile.T
        S = s0_ref[h]
        o = jnp.zeros((SUB, HG_DV), F32)
        for t in range(n_tok):
            S = S * tile_t[:, t:t + 1] + tile_t[:, SUB + t:SUB + t + 1] * v[t:t + 1, :]
            o_t = jnp.sum(S * tile_t[:, 2 * SUB + t:2 * SUB + t + 1], axis=0, keepdims=True)
            o = jnp.where(rows == t, o_t, o)
        s_out_ref[h] = S
        _hgrn_out(o, nw, z_ref, rs, hs, a_ref)


def _hgrn_sample(ub, uf, lb, nw, s0, batch, n_tok):
    return pl.pallas_call(
        functools.partial(_hgrn_step_body, n_tok=n_tok),
        out_shape=(jax.ShapeDtypeStruct((batch * SUB, 512), BF16),
                   jax.ShapeDtypeStruct((batch, HG_HEADS, HG_DK, HG_DV), F32)),
        grid=(batch,),
        in_specs=[pl.BlockSpec((SUB, 512), lambda b: (b, UB_HGQ // 512)),
                  pl.BlockSpec((SUB, 512), lambda b: (b, UB_HGI // 512)),
                  pl.BlockSpec((SUB, 512), lambda b: (b, UF_HGF // 512)),
                  pl.BlockSpec((SUB, 512), lambda b: (b, UB_HGZ // 512)),
                  pl.BlockSpec((1, 512), lambda b: (0, 0)),
                  pl.BlockSpec((1, HG_DV), lambda b: (0, 0)),
                  pl.BlockSpec((None, HG_HEADS, HG_DK, HG_DV), lambda b: (b, 0, 0, 0))],
        out_specs=(pl.BlockSpec((SUB, 512), lambda b: (b, 0)),
                   pl.BlockSpec((None, HG_HEADS, HG_DK, HG_DV), lambda b: (b, 0, 0, 0))),
        compiler_params=_params("arbitrary"),
        name="hgrn_sample",
    )(ub, ub, uf, ub, lb, nw, s0)


def _softmax_update(carry, s, v_fn):
    m, l, acc = carry
    m_new = jnp.maximum(m, jnp.max(s, axis=-1, keepdims=True))
    alpha = jnp.exp(m - m_new)
    p = jnp.exp(s - m_new)
    return m_new, alpha * l + jnp.sum(p, axis=-1, keepdims=True), alpha * acc + v_fn(p.astype(BF16))


def _softmax_init(rows, width=LANES):
    return (jnp.full((rows, 1), NEG, F32), jnp.zeros((rows, 1), F32), jnp.zeros((rows, width), F32))


def _half_mask(x, lane, upper):
    keep = (lane >= 64) if upper else (lane < 64)
    return jnp.where(keep, x, jnp.zeros_like(x))


def _scaled_q(q, scale):
    return (q.astype(F32) * scale).astype(BF16)


def _diff_finish(o, lam, oml, nw, z, r_rows):
    od = o[:r_rows] - lam * o[r_rows:]
    ms = jnp.mean(od * od, axis=-1, keepdims=True)
    return (od * lax.rsqrt(ms + EPS) * nw * oml * _silu(z.astype(F32))).astype(BF16)


def _tflash_loop(lo, hi, score_fn, vt_fn, n, init=None):
    def body(kt, carry):
        m, l, acc = carry
        s, c_row = score_fn(kt)
        m_new = jnp.maximum(m, jnp.max(s, axis=0, keepdims=True) + c_row)
        p = jnp.exp2(s - (m_new - c_row))
        alpha = jnp.exp2(m - m_new)
        return m_new, alpha * l + jnp.sum(p, axis=0, keepdims=True), alpha * acc + _dot(vt_fn(kt), p.astype(BF16))

    if init is None:
        init = (jnp.full((1, n), NEG, F32), jnp.zeros((1, n), F32), jnp.zeros((LANES, n), F32))
    return lax.fori_loop(lo, hi, body, init)


def _fill_transposed(dst3, src_ref, cols, seq, tk):
    for t in range(seq // LANES):
        r0 = t * LANES
        dst3[r0 // tk, :, r0 % tk:r0 % tk + LANES] = src_ref[r0:r0 + LANES, cols].T.astype(BF16)


def _untranspose(x_t, tq):
    return jnp.concatenate([x_t[:, c:c + LANES].T for c in range(0, tq, LANES)], axis=0)


def _rel_t(tk, tq):
    return (_iota((tk, tq), 1) - _iota((tk, tq), 0)).astype(F32)


def _diff_prompt_body(sc_ref, q_ref, kv_ref, z_ref, nw_ref, a_ref, kb, vt, *, tq, tk, seq):
    g = pl.program_id(1)
    i = pl.program_id(2)
    R = DF_HEADS // DF_KV

    @pl.when(i == 0)
    def _():
        kb[...] = kv_ref[:, 0:128].astype(BF16)
        _fill_transposed(vt, kv_ref, slice(128, 256), seq, tk)

    lam, oml = sc_ref[0], sc_ref[1]
    lane = _iota((tq, LANES), 1)
    rel = _rel_t(tk, tq)
    qs, nrel, srow = [], [], []
    for r in range(R):
        slope = sc_ref[2 + g * R + r] * LOG2E
        q = _scaled_q(q_ref[:, r * 128:(r + 1) * 128], DF_HD ** -0.5 * LOG2E)
        qs += [_half_mask(q, lane, False), _half_mask(q, lane, True)]
        nrel += [-slope * rel] * 2
        srow += [jnp.full((1, tq), slope, F32)] * 2
    q4 = jnp.concatenate(qs, axis=0)
    nrel4 = jnp.concatenate(nrel, axis=1)
    srow4 = jnp.concatenate(srow, axis=1)

    def scores(kt, masked):
        off = pl.multiple_of(kt * tk, tk)
        d0 = (i * tq - kt * tk).astype(F32)
        s = _dot_nt(kb[pl.ds(off, tk), :], q4) + nrel4
        if masked:
            s = s + jnp.concatenate([jnp.where(rel + d0 >= 0, 0.0, NEG)] * (2 * R), axis=1)
        return s, -d0 * srow4

    n_full = (i * tq + 1) // tk
    n_t = ((i + 1) * tq + tk - 1) // tk
    carry = _tflash_loop(0, n_full, lambda kt: scores(kt, False), lambda kt: vt[kt], 2 * R * tq)
    _, l, acc = _tflash_loop(n_full, n_t, lambda kt: scores(kt, True), lambda kt: vt[kt], 2 * R * tq, carry)
    o_t = acc / l
    for r in range(R):
        od = _untranspose(o_t[:, 2 * r * tq:(2 * r + 1) * tq] - lam * o_t[:, (2 * r + 1) * tq:(2 * r + 2) * tq], tq)
        ms = jnp.mean(od * od, axis=-1, keepdims=True)
        cs = slice(r * 128, (r + 1) * 128)
        a_ref[:, cs] = (od * lax.rsqrt(ms + EPS) * nw_ref[...] * oml * _silu(z_ref[:, cs].astype(F32))).astype(BF16)


def _diff_prompt(sc, ub, ukv, nw, batch, seq):
    tq = min(512, seq)
    tk = min(256, seq)
    nq = seq // tq
    row = lambda b, g, i: b * nq + i
    return pl.pallas_call(
        functools.partial(_diff_prompt_body, tq=tq, tk=tk, seq=seq),
        out_shape=jax.ShapeDtypeStruct((batch * seq, 512), BF16),
        grid=(batch, DF_KV, nq),
        in_specs=[_smem(),
                  pl.BlockSpec((tq, 256), lambda b, g, i: (row(b, g, i), UB_DFQ // 256 + g)),
                  pl.BlockSpec((seq, 256), lambda b, g, i: (b, KV_DIFF // 256 + g)),
                  pl.BlockSpec((tq, 256), lambda b, g, i: (row(b, g, i), UB_DFZ // 256 + g)),
                  pl.BlockSpec((1, DF_VD), lambda b, g, i: (0, 0))],
        out_specs=pl.BlockSpec((tq, 256), lambda b, g, i: (row(b, g, i), g)),
        scratch_shapes=[pltpu.VMEM((seq, 128), BF16), pltpu.VMEM((seq // tk, 128, tk), BF16)],
        compiler_params=_params("arbitrary", "arbitrary", "arbitrary"),
        name="diff_prompt",
    )(sc, ub, ukv, ub, nw)


def _diff_sample_body(pt_ref, sc_ref, q_ref, new_ref, z_ref, nw_ref, *rest, PP, past, n_tok):
    del pt_ref
    pages = rest[:PP]
    a_ref = rest[PP]
    m_s, l_s, acc_s = rest[PP + 1:]
    c = pl.program_id(1)
    R = DF_HEADS // DF_KV
    rows = 2 * R * SUB

    @pl.when(c == 0)
    def _():
        m_s[...] = jnp.full(m_s.shape, NEG, F32)
        l_s[...] = jnp.zeros(l_s.shape, F32)
        acc_s[...] = jnp.zeros(acc_s.shape, F32)

    lane = _iota((SUB, LANES), 1)
    ridx = _iota((rows, 1), 0)
    tok = ridx & (SUB - 1)
    head = (ridx >> 3) & (R - 1)
    qpos = (past + tok).astype(F32)

    def make_q(g):
        pieces = []
        for m in range(2):
            for r in range(R):
                qs = _scaled_q(q_ref[:, g * 256 + r * 128:g * 256 + (r + 1) * 128], DF_HD ** -0.5)
                pieces.append(_half_mask(qs, lane, m == 1))
        slope = jnp.where(head == 0, sc_ref[2 + g * R], sc_ref[2 + g * R + 1])
        return jnp.concatenate(pieces, axis=0), slope

    def page_rows(p, g, j):
        return pages[p][pl.ds(2 * j + g, PAGE_SIZE, stride=2 * DF_KV), :].astype(BF16)

    for g in range(DF_KV):
        qg, slope = make_q(g)
        k_all = jnp.concatenate([page_rows(p, g, 0) for p in range(PP)], axis=0)
        v_all = jnp.concatenate([page_rows(p, g, 1) for p in range(PP)], axis=0)
        kpos = (c * (PP * PAGE_SIZE) + _iota((rows, PP * PAGE_SIZE), 1)).astype(F32)
        s = _dot_nt(qg, k_all) - slope * (qpos - kpos)
        m, l, acc = _softmax_update((m_s[g], l_s[g], acc_s[g]), s, lambda p: _dot(p, v_all))
        m_s[g], l_s[g], acc_s[g] = m, l, acc

    @pl.when(c == pl.num_programs(1) - 1)
    def _():
        lam, oml = sc_ref[0], sc_ref[1]
        pad = jnp.zeros((LANES - SUB, LANES), F32)
        jcol = _iota((rows, LANES), 1)
        for g in range(DF_KV):
            qg, slope = make_q(g)
            ks = g * 256
            knew = jnp.concatenate([new_ref[:, ks:ks + 128], pad], axis=0).astype(BF16)
            vnew = jnp.concatenate([new_ref[:, ks + 128:ks + 256], pad], axis=0).astype(BF16)
            dist = (tok - jcol).astype(F32)
            s = jnp.where((jcol <= tok) & (jcol < n_tok), _dot_nt(qg, knew) - slope * dist, NEG)
            _, l, acc = _softmax_update((m_s[g], l_s[g], acc_s[g]), s, lambda p: _dot(p, vnew))
            o = acc / l
            for r in range(R):
                o_r = jnp.concatenate([o[r * SUB:(r + 1) * SUB], o[(R + r) * SUB:(R + r + 1) * SUB]], axis=0)
                cs = slice(g * 256 + r * 128, g * 256 + (r + 1) * 128)
                a_ref[:, cs] = _diff_finish(o_r, lam, oml, nw_ref[...], z_ref[:, cs], SUB)


def _diff_sample(l, sc, pt_flat, ub, ukv, nw, cache, batch, n_pages, n_phys, n_tok):
    PP = min(16, n_pages)
    past = n_pages * PAGE_SIZE
    R = DF_HEADS // DF_KV
    rows = 2 * R * SUB

    def page_spec(p):
        return pl.BlockSpec((None, 4 * PAGE_SIZE, LANES),
                            lambda b, c, pt: (l * n_phys + pt[b * n_pages + c * PP + p], 0, 0))

    grid_spec = pltpu.PrefetchScalarGridSpec(
        num_scalar_prefetch=1,
        grid=(batch, n_pages // PP),
        in_specs=[_smem(),
                  pl.BlockSpec((SUB, 512), lambda b, c, pt: (b, UB_DFQ // 512)),
                  pl.BlockSpec((SUB, 512), lambda b, c, pt: (b, KV_DIFF // 512)),
                  pl.BlockSpec((SUB, 512), lambda b, c, pt: (b, UB_DFZ // 512)),
                  pl.BlockSpec((1, DF_VD), lambda b, c, pt: (0, 0))] + [page_spec(p) for p in range(PP)],
        out_specs=pl.BlockSpec((SUB, 512), lambda b, c, pt: (b, 0)),
        scratch_shapes=[pltpu.VMEM((DF_KV, rows, 1), F32), pltpu.VMEM((DF_KV, rows, 1), F32),
                        pltpu.VMEM((DF_KV, rows, DF_VD), F32)])
    return pl.pallas_call(
        functools.partial(_diff_sample_body, PP=PP, past=past, n_tok=n_tok),
        out_shape=jax.ShapeDtypeStruct((batch * SUB, 512), BF16),
        grid_spec=grid_spec,
        compiler_params=_params("arbitrary", "arbitrary"),
        name="diff_sample",
    )(pt_flat, sc, ub, ukv, ub, nw, *([cache] * PP))


def _block_sum(rows_ref, r0, cw):
    return jnp.sum(rows_ref[r0:r0 + CMP_BLK, 0:256] * cw, axis=0, keepdims=True)


def _top_blocks(imp, rowq, lane, n_blocks, n_pick):
    jq = rowq >> 6
    forced = (lane == 0) | (lane == jq) | (lane == jq - 1)
    score = jnp.where((lane <= jq) & (lane < n_blocks), imp + jnp.where(forced, FORCE, 0.0), NEG)
    lanef = lane.astype(F32)
    sel = jnp.zeros(imp.shape, F32)
    picks = []
    for _ in range(n_pick):
        mx = jnp.max(score, axis=-1, keepdims=True)
        am = jnp.min(jnp.where(score == mx, lanef, float(LANES)), axis=-1, keepdims=True)
        hit = lanef == am
        picked = mx > 0.5 * NEG
        sel = jnp.where(hit, jnp.where(picked, 1.0, 0.0), sel)
        score = jnp.where(hit, REMOVED, score)
        picks.append(jnp.where(picked, am, -1.0))
    return sel, picks


def _rank_select(score, n_blocks, n_pick):
    nv = score.shape[0] // SUB
    parts = [score[v * SUB:(v + 1) * SUB] for v in range(nv)]
    rank = [jnp.zeros(parts[0].shape, F32) for _ in range(nv)]
    rows = _iota(parts[0].shape, 0)
    for j in range(n_blocks):
        row = score[j:j + 1, :]
        for v in range(nv):
            if v * SUB > j:
                beats = jnp.where(row >= parts[v], 1.0, 0.0)
            elif v * SUB + SUB - 1 <= j:
                beats = jnp.where(row > parts[v], 1.0, 0.0)
            else:
                beats = jnp.where(rows + v * SUB > j, jnp.where(row >= parts[v], 1.0, 0.0),
                                  jnp.where(row > parts[v], 1.0, 0.0))
            rank[v] = rank[v] + beats
    return jnp.concatenate(
        [jnp.where((rank[v] < n_pick) & (parts[v] > 0.5 * NEG), 1.0, 0.0) for v in range(nv)], axis=0)


def _nsa_prompt_body(sl_ref, q_ref, gn_ref, z_ref, nsa_ref, win_ref, cw_ref, a_ref,
                     kce, kco, vce, vco, ksb, vst, kwb, vwt, *, tq, tk, seq):
    i = pl.program_id(1)
    nb = seq // SEL_BLK
    nbp = -(-nb // SUB) * SUB

    @pl.when(i == 0)
    def _():
        ksb[...] = nsa_ref[:, 256:384].astype(BF16)
        kwb[...] = win_ref[:, 0:128].astype(BF16)
        _fill_transposed(vst, nsa_ref, slice(384, 512), seq, tk)
        _fill_transposed(vwt, win_ref, slice(128, 256), seq, tk)
        kce[...] = jnp.zeros(kce.shape, F32)
        kco[...] = jnp.zeros(kco.shape, F32)
        cw = cw_ref[...]
        rows8 = _iota((SUB, 256), 0)
        v_even, v_odd = [], []
        for t8 in range(nbp // SUB):
            te = jnp.zeros((SUB, 256), F32)
            to = jnp.zeros((SUB, 256), F32)
            for e in range(SUB):
                j = t8 * SUB + e
                if j < nb:
                    te = jnp.where(rows8 == e, _block_sum(nsa_ref, 2 * j * CMP_BLK, cw), te)
                    to = jnp.where(rows8 == e, _block_sum(nsa_ref, (2 * j + 1) * CMP_BLK, cw), to)
            rs = slice(t8 * SUB, (t8 + 1) * SUB)
            kce[rs, :], kco[rs, :] = te[:, :128], to[:, :128]
            v_even.append(te[:, 128:])
            v_odd.append(to[:, 128:])
        pad = [jnp.zeros((LANES - nbp, LANES), F32)] if nbp < LANES else []
        vce[...] = jnp.concatenate(v_even + pad, axis=0).T
        vco[...] = jnp.concatenate(v_odd + pad, axis=0).T

    H = NS_KV * NS_R
    lane = _iota((tq, LANES), 1)
    blk = _iota((nbp, tq), 0)
    qpos = _iota((nbp, tq), 1) + i * tq
    qposf = qpos.astype(F32)
    kc = (kce[0:nbp, :].astype(BF16), kco[0:nbp, :].astype(BF16))
    vct = (vce[:, 0:nbp].astype(BF16), vco[:, 0:nbp].astype(BF16))
    cend = (blk * SEL_BLK + (CMP_BLK - 1), blk * SEL_BLK + (SEL_BLK - 1))
    valid = tuple((qpos >= ce) & (blk < nb) for ce in cend)
    qp = [_scaled_q(q_ref[:, r * 128:(r + 1) * 128], NS_HD ** -0.5 * LOG2E) for r in range(NS_R)]
    slopes = [sl_ref[h] * LOG2E for h in range(H)]
    qms, o_cmp, sel_t = [], [], []
    for g in range(NS_KV):
        imp = jnp.zeros((nbp, tq), F32)
        for r in range(NS_R):
            qm = _half_mask(qp[r], lane, g == 1)
            slope = slopes[g * NS_R + r]
            s = [jnp.where(valid[par], _dot_nt(kc[par], qm) - slope * (qposf - cend[par].astype(F32)), NEG)
                 for par in range(2)]
            mx = jnp.maximum(jnp.max(s[0], axis=0, keepdims=True), jnp.max(s[1], axis=0, keepdims=True))
            e = [jnp.where(valid[par], jnp.exp2(s[par] - mx), 0.0) for par in range(2)]
            den = jnp.maximum(jnp.sum(e[0], axis=0, keepdims=True) + jnp.sum(e[1], axis=0, keepdims=True), TINY)
            p = [e[0] / den, e[1] / den]
            qms.append(qm)
            o_cmp.append(_dot(vct[0], p[0].astype(BF16)) + _dot(vct[1], p[1].astype(BF16)))
            imp = imp + p[0] + p[1]
        jq = qpos >> 6
        forced = (blk == 0) | (blk == jq) | (blk == jq - 1)
        score = jnp.where((blk <= jq) & (blk < nb), imp + jnp.where(forced, FORCE, 0.0), NEG)
        sel_t.append(_rank_select(score, nb, SEL_N).astype(BF16))

    rel = _rel_t(tk, tq)
    nrel = [-slopes[h] * rel for h in range(H)]
    srow8 = jnp.concatenate([jnp.full((1, tq), slopes[h], F32) for h in range(H)], axis=1)
    q8 = jnp.concatenate(qms, axis=0)

    def scores(kt, k_ref, bias_fn):
        off = pl.multiple_of(kt * tk, tk)
        d0 = (i * tq - kt * tk).astype(F32)
        bias = bias_fn(off, rel + d0)
        s = _dot_nt(k_ref[pl.ds(off, tk), :], q8) + jnp.concatenate(
            [nrel[h] + bias[h // NS_R] for h in range(H)], axis=1)
        return s, -d0 * srow8

    def sel_bias(off, dist):
        expand = jnp.where(_iota((tk, nbp), 1) == ((_iota((tk, nbp), 0) + off) >> 6), 1.0, 0.0).astype(BF16)
        return [jnp.where((_dot(expand, st) > 0.5) & (dist >= 0), 0.0, NEG) for st in sel_t]

    def win_bias(off, dist):
        return [jnp.where((dist >= 0) & (dist <= WINDOW), 0.0, NEG)] * NS_KV

    n_t = ((i + 1) * tq + tk - 1) // tk
    _, l_sel, a_sel = _tflash_loop(0, n_t, lambda kt: scores(kt, ksb, sel_bias), lambda kt: vst[kt], H * tq)
    _, l_win, a_win = _tflash_loop(jnp.maximum(i * tq - WINDOW, 0) // tk, n_t,
                                   lambda kt: scores(kt, kwb, win_bias), lambda kt: vwt[kt], H * tq)
    o_sel, o_win = a_sel / l_sel, a_win / l_win
    gates = _sigmoid(gn_ref[...])
    gates_t = jnp.concatenate([gates[c:c + LANES, :].T for c in range(0, tq, LANES)], axis=1)
    half = _iota((LANES, tq), 0) < NS_HD
    for r in range(NS_R):
        og = []
        for g in range(NS_KV):
            h = g * NS_R + r
            cs = slice(h * tq, (h + 1) * tq)
            og.append(gates_t[3 * h:3 * h + 1, :] * o_cmp[h] + gates_t[3 * h + 1:3 * h + 2, :] * o_sel[:, cs]
                      + gates_t[3 * h + 2:3 * h + 3, :] * o_win[:, cs])
        o = _untranspose(jnp.where(half, og[0], og[1]), tq)
        cs = slice(r * 128, (r + 1) * 128)
        a_ref[:, cs] = (o * _silu(z_ref[:, cs].astype(F32))).astype(BF16)


def _nsa_prompt(sl, ub, uf, ukv, cw, batch, seq):
    tq = min(256, seq)
    tk = min(256, seq)
    nq = seq // tq
    row = lambda b, i: b * nq + i
    return pl.pallas_call(
        functools.partial(_nsa_prompt_body, tq=tq, tk=tk, seq=seq),
        out_shape=jax.ShapeDtypeStruct((batch * seq, 512), BF16),
        grid=(batch, nq),
        in_specs=[_smem(),
                  pl.BlockSpec((tq, 512), lambda b, i: (row(b, i), UB_NSQ // 512)),
                  pl.BlockSpec((tq, 128), lambda b, i: (row(b, i), UF_NSG // 128)),
                  pl.BlockSpec((tq, 512), lambda b, i: (row(b, i), UB_NSZ // 512)),
                  pl.BlockSpec((seq, 512), lambda b, i: (b, KV_NSA // 512)),
                  pl.BlockSpec((seq, 256), lambda b, i: (b, KV_WIN // 256)),
                  pl.BlockSpec((CMP_BLK, 256), lambda b, i: (0, 0))],
        out_specs=pl.BlockSpec((tq, 512), lambda b, i: (row(b, i), 0)),
        scratch_shapes=([pltpu.VMEM((LANES, 128), F32)] * 4
                        + [pltpu.VMEM((seq, 128), BF16), pltpu.VMEM((seq // tk, 128, tk), BF16)] * 2),
        compiler_params=_params("arbitrary", "arbitrary"),
        name="nsa_prompt",
    )(sl, ub, uf, ub, ukv, ukv, cw)


def _nsa_sample_a_body(pt_ref, sl_ref, q_ref, gn_ref, buf_ref, wnew_ref, wk_ref, wv_ref, *rest,
                       PP, past, n_tok, n_groups):
    del pt_ref
    pages = rest[:PP]
    part_ref, idx_ref = rest[PP:PP + 2]
    kct, vct = rest[PP + 2:]
    c = pl.program_id(1)
    nsb = past // SEL_BLK

    @pl.when(c == 0)
    def _():
        kct[...] = jnp.zeros(kct.shape, F32)
        vct[...] = jnp.zeros(vct.shape, F32)

    grp = (c * PP) // CMP_PAGES
    w0 = pl.multiple_of(((c * PP) % CMP_PAGES) * PAGE_SIZE, PP * PAGE_SIZE)
    k_all = jnp.concatenate([pages[p][0:128, :].astype(BF16) for p in range(PP)], axis=1)
    v_all = jnp.concatenate([pages[p][128:256, :].astype(BF16) for p in range(PP)], axis=1)
    kct[grp] = kct[grp] + _dot(k_all, wk_ref[pl.ds(w0, PP * PAGE_SIZE), :])
    vct[grp] = vct[grp] + _dot(v_all, wv_ref[pl.ds(w0, PP * PAGE_SIZE), :])

    @pl.when(c == pl.num_programs(1) - 1)
    def _():
        H = NS_KV * NS_R
        rows = H * SUB
        lane8 = _iota((SUB, LANES), 1)
        lane = _iota((rows, LANES), 1)
        tok = _iota((rows, LANES), 0) & (SUB - 1)
        rowq = tok + past
        qposf = rowq.astype(F32)
        qs, sl = [], []
        for g in range(NS_KV):
            for r in range(NS_R):
                qs.append(_half_mask(_scaled_q(q_ref[:, r * 128:(r + 1) * 128], NS_HD ** -0.5), lane8, g == 1))
                sl.append(jnp.full((SUB, 1), sl_ref[g * NS_R + r], F32))
        q8 = jnp.concatenate(qs, axis=0)
        slope = jnp.concatenate(sl, axis=0)
        s, valid = [], []
        for G in range(n_groups):
            blk = G * (LANES // 2) + (lane & (LANES // 2 - 1))
            cend = blk * SEL_BLK + (CMP_BLK - 1) + (lane >> 6) * CMP_BLK
            valid.append((rowq >= cend) & (blk < nsb))
            s.append(jnp.where(valid[G], _dot(q8, kct[G].astype(BF16)) - slope * (qposf - cend.astype(F32)), NEG))
        mx = functools.reduce(jnp.maximum, [jnp.max(x, axis=-1, keepdims=True) for x in s])
        e = [jnp.where(valid[G], jnp.exp(s[G] - mx), 0.0) for G in range(n_groups)]
        den = jnp.maximum(sum(jnp.sum(x, axis=-1, keepdims=True) for x in e), TINY)
        p = [x / den for x in e]
        o_cmp = sum(_dot_nt(p[G].astype(BF16), vct[G].astype(BF16)) for G in range(n_groups))
        imp = []
        for g in range(NS_KV):
            per_tile = []
            for G in range(n_groups):
                x = sum(p[G][(g * NS_R + r) * SUB:(g * NS_R + r + 1) * SUB] for r in range(NS_R))
                per_tile.append(x + pltpu.roll(x, LANES // 2, 1))
            imp.append(per_tile[0] if n_groups == 1 else jnp.where(lane8 < LANES // 2, per_tile[0], per_tile[1]))
        lane16 = _iota((NS_KV * SUB, LANES), 1)
        rowq16 = (_iota((NS_KV * SUB, LANES), 0) & (SUB - 1)) + past
        _, picks = _top_blocks(jnp.concatenate(imp, axis=0), rowq16, lane16, nsb, SEL_N - 1)
        idx = jnp.full((NS_KV * SUB, LANES), -1.0, F32)
        for kk, pk in enumerate(picks):
            idx = jnp.where(lane16 == kk, pk, idx)
        for g in range(NS_KV):
            idx_ref[g] = idx[g * SUB:(g + 1) * SUB].astype(jnp.int32)
        nbuf = buf_ref.shape[1]
        kw = buf_ref[0:128, :].astype(BF16)
        vw = buf_ref[128:256, :].astype(BF16)
        pad = jnp.zeros((LANES - SUB, LANES), F32)
        kwn = jnp.concatenate([wnew_ref[:, 0:128], pad], axis=0).astype(BF16)
        vwn = jnp.concatenate([wnew_ref[:, 128:256], pad], axis=0).astype(BF16)
        tokb = _iota((rows, nbuf), 0) & (SUB - 1)
        dist_b = tokb + (nbuf - _iota((rows, nbuf), 1))
        s_b = jnp.where((dist_b >= 0) & (dist_b <= WINDOW), _dot(q8, kw) - slope * dist_b.astype(F32), NEG)
        s_n = jnp.where((lane <= tok) & (lane < n_tok), _dot_nt(q8, kwn) - slope * (tok - lane).astype(F32), NEG)
        mx = jnp.maximum(jnp.max(s_b, axis=-1, keepdims=True), jnp.max(s_n, axis=-1, keepdims=True))
        p_b, p_n = jnp.exp(s_b - mx), jnp.exp(s_n - mx)
        den = jnp.sum(p_b, axis=-1, keepdims=True) + jnp.sum(p_n, axis=-1, keepdims=True)
        o_win = (_dot_nt(p_b.astype(BF16), vw) + _dot(p_n.astype(BF16), vwn)) / den
        gates = _sigmoid(gn_ref[...])
        for r in range(NS_R):
            og = []
            for g in range(NS_KV):
                h = g * NS_R + r
                hs = slice(h * SUB, (h + 1) * SUB)
                og.append(gates[:, 3 * h:3 * h + 1] * o_cmp[hs] + gates[:, 3 * h + 2:3 * h + 3] * o_win[hs])
            part_ref[:, r * 128:(r + 1) * 128] = jnp.where(lane8 < 64, og[0], og[1])


def _nsa_sample_a(l, sl, pt_flat, ub, uf, ukv, buf_t, wk, wv, cache_t, batch, n_pages, n_phys, n_tok):
    PP = min(16, n_pages)
    past = n_pages * PAGE_SIZE
    nbuf = buf_t.shape[2]
    n_groups = -(-n_pages // CMP_PAGES)
    assert CMP_PAGES % PP == 0

    def page_spec(p):
        return pl.BlockSpec((None, 256, PAGE_SIZE),
                            lambda b, c, pt: (l * n_phys + pt[b * n_pages + c * PP + p], 0, 0))

    grid_spec = pltpu.PrefetchScalarGridSpec(
        num_scalar_prefetch=1,
        grid=(batch, n_pages // PP),
        in_specs=[_smem(),
                  pl.BlockSpec((SUB, 512), lambda b, c, pt: (b, UB_NSQ // 512)),
                  pl.BlockSpec((SUB, 128), lambda b, c, pt: (b, UF_NSG // 128)),
                  pl.BlockSpec((None, 256, nbuf), lambda b, c, pt: (l * batch + b, 0, 0)),
                  pl.BlockSpec((SUB, 256), lambda b, c, pt: (b, KV_WIN // 256)),
                  pl.BlockSpec((CMP_PAGES * PAGE_SIZE, LANES), lambda b, c, pt: (0, 0)),
                  pl.BlockSpec((CMP_PAGES * PAGE_SIZE, LANES), lambda b, c, pt: (0, 0))]
                 + [page_spec(p) for p in range(PP)],
        out_specs=(pl.BlockSpec((SUB, 512), lambda b, c, pt: (b, 0)),
                   pl.BlockSpec((None, NS_KV, SUB, LANES), lambda b, c, pt: (b, 0, 0, 0))),
        scratch_shapes=[pltpu.VMEM((n_groups, LANES, LANES), F32)] * 2)
    return pl.pallas_call(
        functools.partial(_nsa_sample_a_body, PP=PP, past=past, n_tok=n_tok, n_groups=n_groups),
        out_shape=(jax.ShapeDtypeStruct((batch * SUB, 512), F32),
                   jax.ShapeDtypeStruct((batch, NS_KV, SUB, LANES), jnp.int32)),
        grid_spec=grid_spec,
        compiler_params=_params("arbitrary", "arbitrary"),
        name="nsa_sample_a",
    )(pt_flat, sl, ub, uf, buf_t, ukv, wk, wv, *([cache_t] * PP))


def _nsa_sample_b_body(pt_ref, idx_ref, sl_ref, q_ref, gn_ref, z_ref, new_ref, part_ref, *rest, NSEL, past, n_tok):
    del pt_ref
    TT = SUB // NS_R
    blocks = rest[:TT * NSEL]
    a_ref = rest[TT * NSEL]
    acc_scr = rest[TT * NSEL + 1]
    b, g, ts = pl.program_id(0), pl.program_id(1), pl.program_id(2)

    @pl.when((g == 0) & (ts == 0))
    def _():
        acc_scr[...] = part_ref[...]

    lane = _iota((SUB, LANES), 1)
    rows = _iota((SUB, LANES), 0)
    ghalf = (lane >> 6) == g
    rcol = _iota((SUB, 1), 0)
    sh = NS_R.bit_length() - 1
    tcol = ts * TT + (rcol >> sh)
    gates = _sigmoid(gn_ref[...])
    q = jnp.zeros((SUB, LANES), F32)
    slope = jnp.zeros((SUB, 1), F32)
    gate = jnp.zeros((SUB, 1), F32)
    for r in range(NS_R):
        qp = q_ref[:, r * 128:(r + 1) * 128].astype(F32)
        slope = jnp.where((rcol & (NS_R - 1)) == r, sl_ref[g * NS_R + r], slope)
        for tt in range(TT):
            t = ts * TT + tt
            q = jnp.where(rows == tt * NS_R + r, jnp.sum(jnp.where(rows == t, qp, 0.0), axis=0, keepdims=True), q)
            gv = jnp.where((rows == t) & (lane == (g * NS_R + r) * 3 + 1), gates, 0.0)
            gv = jnp.sum(jnp.sum(gv, axis=1, keepdims=True), axis=0, keepdims=True)
            gate = jnp.where(rcol == tt * NS_R + r, gv, gate)
    qb = jnp.where(ghalf, q * NS_HD ** -0.5, 0.0).astype(BF16)
    qpos = past + tcol
    key = ((rows >> sh) << 2) + (lane >> 6)
    bias = []
    for j in range(TT * NSEL):
        blk = idx_ref[((b * NS_KV + g) * n_tok + ts * TT + j // NSEL) * SEL_N + j % NSEL]
        half = jnp.where(blk >= 0, blk & 1, 2)
        kpos = (blk >> 1) * PAGE_SIZE + lane
        bias.append(jnp.where(key == ((j // NSEL) << 2) + half, -slope * (qpos - kpos).astype(F32), NEG))
    k_all = jnp.concatenate([blocks[j][0:128, :].astype(BF16) for j in range(TT * NSEL)], axis=1)
    v_all = jnp.concatenate([blocks[j][128:256, :].astype(BF16) for j in range(TT * NSEL)], axis=1)
    s = _dot(qb, k_all) + jnp.concatenate(bias, axis=1)
    pad = jnp.zeros((LANES - SUB, LANES), F32)
    knew = jnp.concatenate([new_ref[:, 0:128], pad], axis=0).astype(BF16)
    vnew = jnp.concatenate([new_ref[:, 128:256], pad], axis=0).astype(BF16)
    s_new = jnp.where((lane <= tcol) & (lane < n_tok), _dot_nt(qb, knew) - slope * (tcol - lane).astype(F32), NEG)
    mx = jnp.maximum(jnp.max(s, axis=-1, keepdims=True), jnp.max(s_new, axis=-1, keepdims=True))
    p, p_new = jnp.exp(s - mx), jnp.exp(s_new - mx)
    den = jnp.sum(p, axis=-1, keepdims=True) + jnp.sum(p_new, axis=-1, keepdims=True)
    o = gate * (_dot_nt(p.astype(BF16), v_all) + _dot(p_new.astype(BF16), vnew)) / den
    for r in range(NS_R):
        cs = slice(r * 128, (r + 1) * 128)
        upd = acc_scr[:, cs]
        for tt in range(TT):
            upd = upd + jnp.where((rows == ts * TT + tt) & ghalf, o[tt * NS_R + r:tt * NS_R + r + 1, :], 0.0)
        acc_scr[:, cs] = upd

    @pl.when((g == NS_KV - 1) & (ts == pl.num_programs(2) - 1))
    def _():
        a_ref[...] = (acc_scr[...] * _silu(z_ref[...].astype(F32))).astype(BF16)


def _nsa_sample_b(l, sl, pt_flat, idx_flat, ub, uf, ukv, part, cache_t, batch, n_pages, n_phys, n_tok):
    NSEL = SEL_N - 1
    past = n_pages * PAGE_SIZE
    halves = PAGE_SIZE // SEL_BLK

    TT = SUB // NS_R
    assert n_tok % TT == 0

    def blk_spec(j):
        def index(b, g, ts, pt, idx):
            blk = jnp.maximum(idx[((b * NS_KV + g) * n_tok + ts * TT + j // NSEL) * SEL_N + j % NSEL], 0)
            return (l * n_phys + pt[b * n_pages + blk // halves], 1, 0)
        return pl.BlockSpec((None, 256, PAGE_SIZE), index)

    grid_spec = pltpu.PrefetchScalarGridSpec(
        num_scalar_prefetch=2,
        grid=(batch, NS_KV, n_tok // TT),
        in_specs=[_smem(),
                  pl.BlockSpec((SUB, 512), lambda b, g, t, pt, idx: (b, UB_NSQ // 512)),
                  pl.BlockSpec((SUB, 128), lambda b, g, t, pt, idx: (b, UF_NSG // 128)),
                  pl.BlockSpec((SUB, 512), lambda b, g, t, pt, idx: (b, UB_NSZ // 512)),
                  pl.BlockSpec((SUB, 256), lambda b, g, t, pt, idx: (b, (KV_NSA + 256) // 256)),
                  pl.BlockSpec((SUB, 512), lambda b, g, t, pt, idx: (b, 0))]
                 + [blk_spec(j) for j in range(TT * NSEL)],
        out_specs=pl.BlockSpec((SUB, 512), lambda b, g, t, pt, idx: (b, 0)),
        scratch_shapes=[pltpu.VMEM((SUB, 512), F32)])
    return pl.pallas_call(
        functools.partial(_nsa_sample_b_body, NSEL=NSEL, past=past, n_tok=n_tok),
        out_shape=jax.ShapeDtypeStruct((batch * SUB, 512), BF16),
        grid_spec=grid_spec,
        compiler_params=_params("arbitrary", "arbitrary", "arbitrary"),
        name="nsa_sample_b",
    )(pt_flat, idx_flat, sl, ub, uf, ub, ukv, part, *([cache_t] * (TT * NSEL)))


def _final_body(x_ref, gate_ref, gp_ref, ahg_ref, adf_ref, ans_ref, mg_ref, whg_ref, wdf_ref, wns_ref, wo_ref, o_ref):
    d = x_ref.shape[-1]
    merged = (_sigmoid(mg_ref[:, 0:d].astype(F32)) * _dot(ahg_ref[...], whg_ref[...])
              + _sigmoid(mg_ref[:, d:2 * d].astype(F32)) * _dot(adf_ref[...], wdf_ref[...])
              + _sigmoid(mg_ref[:, 2 * d:3 * d].astype(F32)) * _dot(ans_ref[...], wns_ref[...]))
    out = _dot(merged.astype(BF16), wo_ref[...])
    ms = jnp.mean(out * out, axis=-1, keepdims=True)
    o_ref[...] = x_ref[...] + gate_ref[...] * (out * lax.rsqrt(ms + EPS) * gp_ref[...])


def _final(x2, cond, l, g_post, a_hg, a_df, a_ns, ub, w_hg, w_df, w_ns, w_o, tm, rows_per_batch, name):
    rows, d = x2.shape
    wspec = lambda k: pl.BlockSpec((None, k, d), lambda i: (l, 0, 0))
    return pl.pallas_call(
        _final_body,
        out_shape=jax.ShapeDtypeStruct((rows, d), F32),
        grid=(rows // tm,),
        in_specs=[pl.BlockSpec((tm, d), lambda i: (i, 0)),
                  _cond_spec(cond, l, 2, tm, rows_per_batch),
                  pl.BlockSpec((None, 1, d), lambda i: (l, 0, 0)),
                  pl.BlockSpec((tm, 512), lambda i: (i, 0)),
                  pl.BlockSpec((tm, 512), lambda i: (i, 0)),
                  pl.BlockSpec((tm, 512), lambda i: (i, 0)),
                  pl.BlockSpec((tm, 3 * d), lambda i: (i, UB_MG // (3 * d))),
                  wspec(512), wspec(512), wspec(512), wspec(d)],
        out_specs=pl.BlockSpec((tm, d), lambda i: (i, 0)),
        compiler_params=_params("arbitrary"),
        name=name,
    )(x2, cond, g_post, a_hg, a_df, a_ns, ub, w_hg, w_df, w_ns, w_o)


def _alibi(n):
    return [2.0 ** (-8.0 * (h + 1) / n) for h in range(n)]


def kernel(x_prompt, x_sample, cache_diff, cache_nsa, cache_nsa_win, state_hgrn, page_table, c_prompt, c_sample,
           w_cond, b_cond, g_pre, g_post, w_in, hg_lb, hg_norm, df_lam, df_norm, ns_cmp,
           w_hg_out, w_df_out, w_ns_out, w_out):
    depth, d, _ = w_in.shape
    bp, seq, _ = x_prompt.shape
    bs, n_tok, _ = x_sample.shape
    n_pages = page_table.shape[1]
    n_phys = cache_diff.shape[1]
    past = n_pages * PAGE_SIZE
    assert d == 1024 and seq % 128 == 0 and n_tok <= SUB and n_pages % 4 == 0
    assert cache_nsa_win.shape[2] == WINDOW and past >= WINDOW and past // SEL_BLK <= LANES

    kv_cols, uf_cols, ub_cols, pair = _column_layout()
    w_kv = jnp.take(w_in, kv_cols, axis=2).astype(BF16)
    w_uf = jnp.where(uf_cols >= 0, jnp.take(w_in, np.maximum(uf_cols, 0), axis=2), 0.0).astype(BF16)
    w_ub = jnp.take(w_in, ub_cols, axis=2).astype(BF16)
    w_hg = w_hg_out.astype(BF16)
    w_df = w_df_out.astype(BF16)
    w_ns = jnp.take(w_ns_out, pair, axis=1).astype(BF16)
    w_o = w_out.astype(BF16)
    g_pre3 = g_pre.reshape(depth, 1, d)
    g_post3 = g_post.reshape(depth, 1, d)
    lb_w = jax.nn.softmax(hg_lb.astype(F32), axis=0)
    lower = jnp.cumsum(lb_w, axis=0) - lb_w[0]
    cw_all = jax.nn.softmax(ns_cmp.astype(F32), axis=-1)
    cw_tile = jnp.repeat(jnp.swapaxes(cw_all, 1, 2), 128, axis=2)
    lv = df_lam.astype(F32)
    lam_init = jnp.asarray([0.8 - 0.6 * math.exp(-0.3 * l) for l in range(depth)], F32)
    lam = jnp.exp(jnp.sum(lv[:, 0] * lv[:, 1], axis=-1)) - jnp.exp(jnp.sum(lv[:, 2] * lv[:, 3], axis=-1)) + lam_init
    df_sc = jnp.concatenate([lam[:, None], 1.0 - lam_init[:, None],
                             jnp.broadcast_to(jnp.asarray(_alibi(DF_HEADS), F32), (depth, DF_HEADS))], axis=1)
    ns_sl = jnp.asarray(_alibi(NS_HEADS), F32)

    bc = bp + bs
    bc_pad = -(-bc // SUB) * SUB
    c_all = jnp.concatenate([c_prompt, c_sample, jnp.zeros((bc_pad - bc, d), F32)], axis=0)
    cond = _cond_all(c_all, w_cond, b_cond)
    cond_p = cond[:, :bp].reshape(depth, bp, 1, 3 * d)
    cond_s = jnp.repeat(cond[:, bp:bc], SUB, axis=1)

    pt_flat = page_table.reshape(-1).astype(jnp.int32)
    cache_diff2 = cache_diff.reshape(depth, n_phys, PAGE_SIZE, DF_KV, 2, LANES).transpose(0, 1, 2, 4, 3, 5)
    cache_diff2 = cache_diff2.reshape(depth * n_phys, 4 * PAGE_SIZE, LANES)
    cache_nsa_t = cache_nsa.transpose(0, 1, 3, 4, 5, 2).reshape(depth * n_phys, 512, PAGE_SIZE)
    win_buf_t = cache_nsa_win.transpose(0, 1, 3, 4, 5, 2).reshape(depth * bs, 256, WINDOW)
    pos = np.arange(PAGE_SIZE)
    nn = pos // CMP_BLK
    tgt = (nn % 2)[None, :] * (LANES // 2) + 2 * np.arange(CMP_PAGES)[:, None] + (nn // 2)[None, :]
    place = jnp.asarray(tgt[:, :, None] == np.arange(LANES)[None, None, :], F32)
    cw_pos = cw_all[:, :, pos % CMP_BLK]
    w_cmp = (place[None, None] * cw_pos[:, :, None, :, None]).astype(BF16)
    w_cmp = w_cmp.reshape(depth, 2, CMP_PAGES * PAGE_SIZE, LANES)

    xp = x_prompt.reshape(bp * seq, d)
    xs = jnp.pad(x_sample, ((0, 0), (0, SUB - n_tok), (0, 0))).reshape(bs * SUB, d)
    zeros_state = jnp.zeros((bp, HG_HEADS, HG_DK, HG_DV), F32)
    tm_p = min(1024, seq)
    tm_f = min(512, seq)
    rows_s = bs * SUB

    outs = {k: [] for k in ('dp', 'ds', 'np', 'ns', 'wp', 'ws', 'hp', 'hs')}
    for l in range(depth):
        lb = lower[l].reshape(1, 512)
        nw_hg = hg_norm[l].reshape(1, HG_DV)
        nw_df = df_norm[l].reshape(1, DF_VD)

        ukv = _inproj(xp, cond_p, l, g_pre3, w_kv, KV_W, F32, tm_p, seq, "inproj_kv_p")
        uf = _inproj(xp, cond_p, l, g_pre3, w_uf, UF_W, F32, tm_p, seq, "inproj_f_p")
        ub = _inproj(xp, cond_p, l, g_pre3, w_ub, UB_W // 2, BF16, tm_p, seq, "inproj_b_p")
        a_hg, s_p = _hgrn_prompt(ub, uf, lb, nw_hg, zeros_state, bp, seq)
        a_df = _diff_prompt(df_sc[l], ub, ukv, nw_df, bp, seq)
        a_ns = _nsa_prompt(ns_sl, ub, uf, ukv, cw_tile[l], bp, seq)
        xp = _final(xp, cond_p, l, g_post3, a_hg, a_df, a_ns, ub, w_hg, w_df, w_ns, w_o, tm_f, seq, "final_p")
        ukv3 = ukv.reshape(bp, seq, KV_W)
        outs['dp'].append(ukv3[:, :, KV_DIFF:KV_NSA].reshape(bp, seq, DF_KV, 256))
        outs['np'].append(ukv3[:, :, KV_NSA:KV_WIN].reshape(bp, seq, 4, NS_KV, NS_HD))
        outs['wp'].append(ukv3[:, seq - min(WINDOW, seq):, KV_WIN:].reshape(bp, min(WINDOW, seq), 2, NS_KV, NS_HD))
        outs['hp'].append(s_p)

        ukv = _inproj(xs, cond_s, l, g_pre3, w_kv, KV_W, F32, rows_s, SUB, "inproj_kv_s")
        uf = _inproj(xs, cond_s, l, g_pre3, w_uf, UF_W, F32, rows_s, SUB, "inproj_f_s")
        ub = _inproj(xs, cond_s, l, g_pre3, w_ub, 512, BF16, rows_s, SUB, "inproj_b_s")
        a_hg, s_s = _hgrn_sample(ub, uf, lb, nw_hg, state_hgrn[l], bs, n_tok)
        a_df = _diff_sample(l, df_sc[l], pt_flat, ub, ukv, nw_df, cache_diff2, bs, n_pages, n_phys, n_tok)
        part, idx = _nsa_sample_a(l, ns_sl, pt_flat, ub, uf, ukv, win_buf_t, w_cmp[l, 0], w_cmp[l, 1], cache_nsa_t,
                                  bs, n_pages, n_phys, n_tok)
        idx_flat = idx[:, :, :n_tok, :SEL_N].reshape(-1)
        a_ns = _nsa_sample_b(l, ns_sl, pt_flat, idx_flat, ub, uf, ukv, part, cache_nsa_t,
                             bs, n_pages, n_phys, n_tok)
        xs = _final(xs, cond_s, l, g_post3, a_hg, a_df, a_ns, ub, w_hg, w_df, w_ns, w_o, rows_s, SUB, "final_s")
        ukv3 = ukv.reshape(bs, SUB, KV_W)[:, :n_tok]
        outs['ds'].append(ukv3[:, :, KV_DIFF:KV_NSA].reshape(bs, n_tok, DF_KV, 256))
        outs['ns'].append(ukv3[:, :, KV_NSA:KV_WIN].reshape(bs, n_tok, 4, NS_KV, NS_HD))
        new_w = ukv3[:, :, KV_WIN:].reshape(bs, n_tok, 2, NS_KV, NS_HD)
        outs['ws'].append(jnp.concatenate([cache_nsa_win[l][:, n_tok:], new_w], axis=1))
        outs['hs'].append(s_s)

    st = {k: jnp.stack(v) for k, v in outs.items()}
    y_p = xp.reshape(bp, seq, d)
    y_s = xs.reshape(bs, SUB, d)[:, :n_tok]
    return (y_p, y_s, st['dp'], st['ds'], st['np'], st['ns'], st['wp'], st['ws'], st['hp'], st['hs'])
```

```python
import functools
import math

import numpy as np
import jax
import jax.numpy as jnp
from jax import lax
from jax.experimental import pallas as pl
from jax.experimental.pallas import tpu as pltpu

F32 = jnp.float32
BF16 = jnp.bfloat16

PAGE_SIZE = 128
HG_HEADS, HG_DK, HG_DV = 4, 128, 128
HG_CHUNK = 64
DF_HEADS, DF_KV, DF_HD = 4, 2, 64
DF_VD = 2 * DF_HD
NS_HEADS, NS_KV, NS_HD = 8, 2, 64
NS_R = NS_HEADS // NS_KV
CMP_BLK, SEL_BLK, SEL_N, WINDOW = 32, 64, 8, 512
EPS, NEG, TINY, FORCE = 1e-6, -1e30, 1e-30, 1e4
REMOVED = -3e38
EXP_CLAMP = 80.0
LOG2E = math.log2(math.e)
LANES = 128
SUB = 8
CMP_PAGES = 32
VMEM_LIMIT = 48 * 1024 * 1024

IN_SPLITS = (
    ('hg_q', 512), ('hg_f', 512), ('hg_i', 512), ('hg_z', 512),
    ('df_q', 512), ('df_k', 256), ('df_v', 256), ('df_z', 512),
    ('ns_q', 512), ('ns_kvc', 256), ('ns_kvs', 256), ('ns_kvw', 256), ('ns_g', 24), ('ns_z', 512),
    ('merge', 3072),
)

KV_DIFF, KV_NSA, KV_WIN, KV_W = 0, 512, 1024, 1280
UF_HGF, UF_NSG, UF_W = 0, 512, 640
UB_MG, UB_HGQ, UB_HGI, UB_DFQ, UB_NSQ, UB_HGZ, UB_DFZ, UB_NSZ, UB_W = (
    0, 3072, 3584, 4096, 4608, 5120, 5632, 6144, 6656)


def _column_layout():
    off, o = {}, 0
    for n, w in IN_SPLITS:
        off[n] = o
        o += w
    rng = lambda n, a, b: list(range(off[n] + a, off[n] + b))
    kv = []
    for g in range(DF_KV):
        kv += rng('df_k', g * 128, (g + 1) * 128) + rng('df_v', g * 128, (g + 1) * 128)
    kv += rng('ns_kvc', 0, 256) + rng('ns_kvs', 0, 256) + rng('ns_kvw', 0, 256)
    uf = rng('hg_f', 0, 512) + rng('ns_g', 0, 24) + [-1] * 104
    pair = [g * 256 + r * 64 + d for r in range(NS_R) for g in range(NS_KV) for d in range(NS_HD)]
    ub = (rng('merge', 0, 3072) + rng('hg_q', 0, 512) + rng('hg_i', 0, 512) + rng('df_q', 0, 512)
          + [off['ns_q'] + p for p in pair] + rng('hg_z', 0, 512) + rng('df_z', 0, 512)
          + [off['ns_z'] + p for p in pair])
    assert len(kv) == KV_W and len(uf) == UF_W and len(ub) == UB_W
    return np.asarray(kv), np.asarray(uf), np.asarray(ub), np.asarray(pair)


def _dot(a, b):
    return jnp.dot(a, b, preferred_element_type=F32)


def _dot_nt(a, b):
    return lax.dot_general(a, b, (((1,), (1,)), ((), ())), preferred_element_type=F32)


def _sigmoid(x):
    return 1.0 / (1.0 + jnp.exp(-x))


def _silu(x):
    return x * _sigmoid(x)


def _iota(shape, dim):
    return lax.broadcasted_iota(jnp.int32, shape, dim)


def _params(*sem):
    return pltpu.CompilerParams(dimension_semantics=sem, vmem_limit_bytes=VMEM_LIMIT)


def _smem():
    return pl.BlockSpec(memory_space=pltpu.SMEM)


def _cond_body(c_ref, w_ref, b_ref, o_ref):
    a = _silu(c_ref[...])
    a_hi = a.astype(BF16)
    a_lo = (a - a_hi.astype(F32)).astype(BF16)
    w = w_ref[...]
    w_hi = w.astype(BF16)
    w_lo = (w - w_hi.astype(F32)).astype(BF16)
    o_ref[...] = _dot(a_hi, w_hi) + (_dot(a_hi, w_lo) + _dot(a_lo, w_hi)) + b_ref[...]


def _cond_all(c_all, w_cond, b_cond):
    depth, d, _ = w_cond.shape
    bc = c_all.shape[0]
    return pl.pallas_call(
        _cond_body,
        out_shape=jax.ShapeDtypeStruct((depth, bc, 3 * d), F32),
        grid=(depth, 3),
        in_specs=[pl.BlockSpec((bc, d), lambda l, k: (0, 0)),
                  pl.BlockSpec((None, d, d), lambda l, k: (l, 0, k)),
                  pl.BlockSpec((None, 1, d), lambda l, k: (l, 0, k))],
        out_specs=pl.BlockSpec((None, bc, d), lambda l, k: (l, 0, k)),
        compiler_params=_params("arbitrary", "arbitrary"),
        name="cond",
    )(c_all, w_cond, b_cond.reshape(depth, 1, 3 * d))


def _inproj_body(x_ref, sh_ref, sc_ref, gp_ref, w_ref, o_ref, h_scr):
    @pl.when(pl.program_id(1) == 0)
    def _():
        x = x_ref[...]
        ms = jnp.mean(x * x, axis=-1, keepdims=True)
        y = x * lax.rsqrt(ms + EPS) * gp_ref[...]
        h_scr[...] = (y * (1.0 + sc_ref[...]) + sh_ref[...]).astype(BF16)

    o_ref[...] = _dot(h_scr[...], w_ref[...]).astype(o_ref.dtype)


def _inproj_f32_body(x_ref, sh_ref, sc_ref, gp_ref, w_ref, okv_ref, of_ref):
    x = x_ref[...]
    ms = jnp.mean(x * x, axis=-1, keepdims=True)
    y = x * lax.rsqrt(ms + EPS) * gp_ref[...]
    h = (y * (1.0 + sc_ref[...]) + sh_ref[...]).astype(BF16)
    res = _dot(h, w_ref[...])
    okv_ref[...] = res[:, :KV_W]
    of_ref[...] = res[:, KV_W:]


def _inproj_f32(x2, cond, l, g_pre, w, tm, rows_per_batch, name):
    rows, d = x2.shape
    return pl.pallas_call(
        _inproj_f32_body,
        out_shape=(jax.ShapeDtypeStruct((rows, KV_W), F32), jax.ShapeDtypeStruct((rows, UF_W), F32)),
        grid=(rows // tm,),
        in_specs=[pl.BlockSpec((tm, d), lambda i: (i, 0)),
                  _cond_spec(cond, l, 0, tm, rows_per_batch),
                  _cond_spec(cond, l, 1, tm, rows_per_batch),
                  pl.BlockSpec((None, 1, d), lambda i: (l, 0, 0)),
                  pl.BlockSpec((None, d, KV_W + UF_W), lambda i: (l, 0, 0))],
        out_specs=(pl.BlockSpec((tm, KV_W), lambda i: (i, 0)), pl.BlockSpec((tm, UF_W), lambda i: (i, 0))),
        compiler_params=_params("arbitrary"),
        name=name,
    )(x2, cond, cond, g_pre, w)


def _cond_spec(cond, l, k, tm, rows_per_batch):
    d = cond.shape[-1] // 3
    if cond.ndim == 4:
        tiles = rows_per_batch // tm
        return pl.BlockSpec((None, None, 1, d), lambda i, *_: (l, i // tiles, 0, k))
    return pl.BlockSpec((None, tm, d), lambda i, *_: (l, i, k))


def _inproj(x2, cond, l, g_pre, w, tn, out_dtype, tm, rows_per_batch, name):
    rows, d = x2.shape
    n = w.shape[-1]
    return pl.pallas_call(
        _inproj_body,
        out_shape=jax.ShapeDtypeStruct((rows, n), out_dtype),
        grid=(rows // tm, n // tn),
        in_specs=[pl.BlockSpec((tm, d), lambda i, j: (i, 0)),
                  _cond_spec(cond, l, 0, tm, rows_per_batch),
                  _cond_spec(cond, l, 1, tm, rows_per_batch),
                  pl.BlockSpec((None, 1, d), lambda i, j: (l, 0, 0)),
                  pl.BlockSpec((None, d, tn), lambda i, j: (l, 0, j))],
        out_specs=pl.BlockSpec((tm, tn), lambda i, j: (i, j)),
        scratch_shapes=[pltpu.VMEM((tm, d), BF16)],
        compiler_params=_params("arbitrary", "arbitrary"),
        name=name,
    )(x2, cond, cond, g_pre, w)


def _hgrn_gates(fr, lbh):
    e = jnp.exp(-jnp.abs(fr))
    inv = 1.0 / (1.0 + e)
    pos = fr >= 0
    sg = jnp.where(pos, inv, e * inv)
    sgn = jnp.where(pos, e * inv, inv)
    fg = lbh + (1.0 - lbh) * sg
    return fg, jnp.log(jnp.maximum(fg, TINY)), (1.0 - lbh) * sgn


def _hgrn_out(o, nw, z_ref, rs, hs, a_ref):
    ms = jnp.mean(o * o, axis=-1, keepdims=True)
    on = o * lax.rsqrt(ms + EPS) * nw
    a_ref[rs, hs] = (on * _silu(z_ref[rs, hs].astype(F32))).astype(BF16)


def _hgrn_chunk_body(q_ref, v_ref, f_ref, z_ref, lb_ref, nw_ref, s0_ref, a_ref, s_out_ref, s_scr, *, C, n_chunks):
    t = pl.program_id(1)

    @pl.when(t == 0)
    def _():
        s_scr[...] = s0_ref[...]

    ts = C * n_chunks
    shift = C.bit_length() - 1
    ri, ci = _iota((ts, ts), 0), _iota((ts, ts), 1)
    tri = (ri >= ci) & ((ri >> shift) == (ci >> shift))
    tri_b = jnp.where(tri, 1.0, 0.0).astype(BF16)
    mid = C // 2
    nw = nw_ref[...]
    _, g, k = _hgrn_gates(f_ref[...], lb_ref[...])
    q = _silu(q_ref[...].astype(F32))
    g1 = g.astype(BF16)
    r1 = g - g1.astype(F32)
    g2 = r1.astype(BF16)
    g3 = (r1 - g2.astype(F32)).astype(BF16)
    G = _dot(tri_b, g1) + _dot(tri_b, g2) + _dot(tri_b, g3)

    def per_chunk_row(idx):
        return jnp.concatenate(
            [jnp.broadcast_to(G[c * C + idx:c * C + idx + 1, :], (C, G.shape[1])) for c in range(n_chunks)], axis=0)

    g_mid, g_last = per_chunk_row(mid - 1), per_chunk_row(C - 1)
    qe = (q * jnp.exp(jnp.minimum(G - g_mid, EXP_CLAMP))).astype(BF16)
    ke = (k * jnp.exp(jnp.minimum(g_mid - G, EXP_CLAMP))).astype(BF16)
    qg = (q * jnp.exp(G)).astype(BF16)
    kd = k * jnp.exp(g_last - G)
    pad = jnp.zeros((LANES - C - SUB, HG_DK), F32)
    for h in range(HG_HEADS):
        hs = slice(h * HG_DK, (h + 1) * HG_DK)
        att = jnp.where(tri, _dot_nt(qe[:, hs], ke[:, hs]), 0.0)
        o_intra = _dot(att.astype(BF16), v_ref[:, hs])
        S = s_scr[h]
        o_inter = []
        for c in range(n_chunks):
            rs = slice(c * C, (c + 1) * C)
            o_inter.append(_dot(qg[rs, hs], S.astype(BF16)))
            decay = jnp.exp(G[(c + 1) * C - 1:(c + 1) * C, hs])
            tile_t = jnp.concatenate([kd[rs, hs], jnp.broadcast_to(decay, (SUB, HG_DK)), pad], axis=0).T
            S = tile_t[:, C:C + 1] * S + _dot(tile_t[:, :C].astype(BF16), v_ref[rs, hs])
        s_scr[h] = S
        _hgrn_out(o_intra + jnp.concatenate(o_inter, axis=0), nw, z_ref, slice(0, ts), hs, a_ref)

    @pl.when(t == pl.num_programs(1) - 1)
    def _():
        s_out_ref[...] = s_scr[...]


def _hgrn_prompt(ub, uf, lb, nw, s0, batch, seq):
    ts = min(256, seq)
    C = math.gcd(ts, HG_CHUNK)
    nt = seq // ts
    row = lambda b, t: b * nt + t
    return pl.pallas_call(
        functools.partial(_hgrn_chunk_body, C=C, n_chunks=ts // C),
        out_shape=(jax.ShapeDtypeStruct((batch * seq, 512), BF16),
                   jax.ShapeDtypeStruct((batch, HG_HEADS, HG_DK, HG_DV), F32)),
        grid=(batch, nt),
        in_specs=[pl.BlockSpec((ts, 512), lambda b, t: (row(b, t), UB_HGQ // 512)),
                  pl.BlockSpec((ts, 512), lambda b, t: (row(b, t), UB_HGI // 512)),
                  pl.BlockSpec((ts, 512), lambda b, t: (row(b, t), UF_HGF // 512)),
                  pl.BlockSpec((ts, 512), lambda b, t: (row(b, t), UB_HGZ // 512)),
                  pl.BlockSpec((1, 512), lambda b, t: (0, 0)),
                  pl.BlockSpec((1, HG_DV), lambda b, t: (0, 0)),
                  pl.BlockSpec((None, HG_HEADS, HG_DK, HG_DV), lambda b, t: (b, 0, 0, 0))],
        out_specs=(pl.BlockSpec((ts, 512), lambda b, t: (row(b, t), 0)),
                   pl.BlockSpec((None, HG_HEADS, HG_DK, HG_DV), lambda b, t: (b, 0, 0, 0))),
        scratch_shapes=[pltpu.VMEM((HG_HEADS, HG_DK, HG_DV), F32)],
        compiler_params=_params("arbitrary", "arbitrary"),
        name="hgrn_prompt",
    )(ub, ub, uf, ub, lb, nw, s0)


def _hgrn_step_body(q_ref, v_ref, f_ref, z_ref, lb_ref, nw_ref, s0_ref, a_ref, s_out_ref, *, n_tok):
    rows = _iota((SUB, HG_DV), 0)
    nw = nw_ref[...]
    rs = slice(0, SUB)
    for h in range(HG_HEADS):
        hs = slice(h * HG_DK, (h + 1) * HG_DK)
        fg, _, k = _hgrn_gates(f_ref[:, hs], lb_ref[:, hs])
        q = _silu(q_ref[:, hs].astype(F32))
        v = v_ref[:, hs].astype(F32)
        tile = jnp.concatenate([fg, k, q, jnp.zeros((LANES - 3 * SUB, HG_DK), F32)], axis=0)
        tile_t = tile.T
        S = s0_ref[h]
        o = jnp.zeros((SUB, HG_DV), F32)
        for t in range(n_tok):
            S = S * tile_t[:, t:t + 1] + tile_t[:, SUB + t:SUB + t + 1] * v[t:t + 1, :]
            o_t = jnp.sum(S * tile_t[:, 2 * SUB + t:2 * SUB + t + 1], axis=0, keepdims=True)
            o = jnp.where(rows == t, o_t, o)
        s_out_ref[h] = S
        _hgrn_out(o, nw, z_ref, rs, hs, a_ref)


def _hgrn_sample(ub, uf, lb, nw, s0, batch, n_tok):
    return pl.pallas_call(
        functools.partial(_hgrn_step_body, n_tok=n_tok),
        out_shape=(jax.ShapeDtypeStruct((batch * SUB, 512), BF16),
                   jax.ShapeDtypeStruct((batch, HG_HEADS, HG_DK, HG_DV), F32)),
        grid=(batch,),
        in_specs=[pl.BlockSpec((SUB, 512), lambda b: (b, UB_HGQ // 512)),
                  pl.BlockSpec((SUB, 512), lambda b: (b, UB_HGI // 512)),
                  pl.BlockSpec((SUB, 512), lambda b: (b, UF_HGF // 512)),
                  pl.BlockSpec((SUB, 512), lambda b: (b, UB_HGZ // 512)),
                  pl.BlockSpec((1, 512), lambda b: (0, 0)),
                  pl.BlockSpec((1, HG_DV), lambda b: (0, 0)),
                  pl.BlockSpec((None, HG_HEADS, HG_DK, HG_DV), lambda b: (b, 0, 0, 0))],
        out_specs=(pl.BlockSpec((SUB, 512), lambda b: (b, 0)),
                   pl.BlockSpec((None, HG_HEADS, HG_DK, HG_DV), lambda b: (b, 0, 0, 0))),
        compiler_params=_params("arbitrary"),
        name="hgrn_sample",
    )(ub, ub, uf, ub, lb, nw, s0)


def _softmax_update(carry, s, v_fn):
    m, l, acc = carry
    m_new = jnp.maximum(m, jnp.max(s, axis=-1, keepdims=True))
    alpha = jnp.exp(m - m_new)
    p = jnp.exp(s - m_new)
    return m_new, alpha * l + jnp.sum(p, axis=-1, keepdims=True), alpha * acc + v_fn(p.astype(BF16))


def _softmax_init(rows, width=LANES):
    return (jnp.full((rows, 1), NEG, F32), jnp.zeros((rows, 1), F32), jnp.zeros((rows, width), F32))


def _half_mask(x, lane, upper):
    keep = (lane >= 64) if upper else (lane < 64)
    return jnp.where(keep, x, jnp.zeros_like(x))


def _scaled_q(q, scale):
    return (q.astype(F32) * scale).astype(BF16)


def _diff_finish(o, lam, oml, nw, z, r_rows):
    od = o[:r_rows] - lam * o[r_rows:]
    ms = jnp.mean(od * od, axis=-1, keepdims=True)
    return (od * lax.rsqrt(ms + EPS) * nw * oml * _silu(z.astype(F32))).astype(BF16)


def _tflash_loop(lo, hi, score_fn, vt_fn, n, init=None):
    def body(kt, carry):
        m, l, acc = carry
        s, c_row = score_fn(kt)
        m_new = jnp.maximum(m, jnp.max(s, axis=0, keepdims=True) + c_row)
        p = jnp.exp2(s - (m_new - c_row))
        alpha = jnp.exp2(m - m_new)
        return m_new, alpha * l + jnp.sum(p, axis=0, keepdims=True), alpha * acc + _dot(vt_fn(kt), p.astype(BF16))

    if init is None:
        init = (jnp.full((1, n), NEG, F32), jnp.zeros((1, n), F32), jnp.zeros((LANES, n), F32))
    return lax.fori_loop(lo, hi, body, init)


def _fill_transposed(dst3, src_ref, cols, seq, tk):
    for t in range(seq // LANES):
        r0 = t * LANES
        dst3[r0 // tk, :, r0 % tk:r0 % tk + LANES] = src_ref[r0:r0 + LANES, cols].T.astype(BF16)


def _untranspose(x_t, tq):
    return jnp.concatenate([x_t[:, c:c + LANES].T for c in range(0, tq, LANES)], axis=0)


def _rel_t(tk, tq):
    return (_iota((tk, tq), 1) - _iota((tk, tq), 0)).astype(F32)


def _diff_prompt_body(sc_ref, q_ref, kv_ref, z_ref, nw_ref, a_ref, kb, vt, *, tq, tk, seq):
    g = pl.program_id(1)
    i = pl.program_id(2)
    R = DF_HEADS // DF_KV

    @pl.when(i == 0)
    def _():
        kb[...] = kv_ref[:, 0:128].astype(BF16)
        _fill_transposed(vt, kv_ref, slice(128, 256), seq, tk)

    lam, oml = sc_ref[0], sc_ref[1]
    lane = _iota((tq, LANES), 1)
    rel = _rel_t(tk, tq)
    qs, nrel, srow = [], [], []
    for r in range(R):
        slope = sc_ref[2 + g * R + r] * LOG2E
        q = _scaled_q(q_ref[:, r * 128:(r + 1) * 128], DF_HD ** -0.5 * LOG2E)
        qs += [_half_mask(q, lane, False), _half_mask(q, lane, True)]
        nrel += [-slope * rel] * 2
        srow += [jnp.full((1, tq), slope, F32)] * 2
    q4 = jnp.concatenate(qs, axis=0)
    nrel4 = jnp.concatenate(nrel, axis=1)
    srow4 = jnp.concatenate(srow, axis=1)

    def scores(kt, masked):
        off = pl.multiple_of(kt * tk, tk)
        d0 = (i * tq - kt * tk).astype(F32)
        s = _dot_nt(kb[pl.ds(off, tk), :], q4) + nrel4
        if masked:
            s = s + jnp.concatenate([jnp.where(rel + d0 >= 0, 0.0, NEG)] * (2 * R), axis=1)
        return s, -d0 * srow4

    n_full = (i * tq + 1) // tk
    n_t = ((i + 1) * tq + tk - 1) // tk
    carry = _tflash_loop(0, n_full, lambda kt: scores(kt, False), lambda kt: vt[kt], 2 * R * tq)
    _, l, acc = _tflash_loop(n_full, n_t, lambda kt: scores(kt, True), lambda kt: vt[kt], 2 * R * tq, carry)
    o_t = acc / l
    for r in range(R):
        od = _untranspose(o_t[:, 2 * r * tq:(2 * r + 1) * tq] - lam * o_t[:, (2 * r + 1) * tq:(2 * r + 2) * tq], tq)
        ms = jnp.mean(od * od, axis=-1, keepdims=True)
        cs = slice(r * 128, (r + 1) * 128)
        a_ref[:, cs] = (od * lax.rsqrt(ms + EPS) * nw_ref[...] * oml * _silu(z_ref[:, cs].astype(F32))).astype(BF16)


def _diff_prompt(sc, ub, ukv, nw, batch, seq):
    tq = min(512, seq)
    tk = min(256, seq)
    nq = seq // tq
    row = lambda b, g, i: b * nq + i
    return pl.pallas_call(
        functools.partial(_diff_prompt_body, tq=tq, tk=tk, seq=seq),
        out_shape=jax.ShapeDtypeStruct((batch * seq, 512), BF16),
        grid=(batch, DF_KV, nq),
        in_specs=[_smem(),
                  pl.BlockSpec((tq, 256), lambda b, g, i: (row(b, g, i), UB_DFQ // 256 + g)),
                  pl.BlockSpec((seq, 256), lambda b, g, i: (b, KV_DIFF // 256 + g)),
                  pl.BlockSpec((tq, 256), lambda b, g, i: (row(b, g, i), UB_DFZ // 256 + g)),
                  pl.BlockSpec((1, DF_VD), lambda b, g, i: (0, 0))],
        out_specs=pl.BlockSpec((tq, 256), lambda b, g, i: (row(b, g, i), g)),
        scratch_shapes=[pltpu.VMEM((seq, 128), BF16), pltpu.VMEM((seq // tk, 128, tk), BF16)],
        compiler_params=_params("arbitrary", "arbitrary", "arbitrary"),
        name="diff_prompt",
    )(sc, ub, ukv, ub, nw)


def _diff_sample_body(pt_ref, sc_ref, q_ref, new_ref, z_ref, nw_ref, *rest, PP, past, n_tok):
    del pt_ref
    pages = rest[:PP]
    a_ref = rest[PP]
    m_s, l_s, acc_s = rest[PP + 1:]
    c = pl.program_id(1)
    R = DF_HEADS // DF_KV
    rows = 2 * R * SUB

    @pl.when(c == 0)
    def _():
        m_s[...] = jnp.full(m_s.shape, NEG, F32)
        l_s[...] = jnp.zeros(l_s.shape, F32)
        acc_s[...] = jnp.zeros(acc_s.shape, F32)

    lane = _iota((SUB, LANES), 1)
    ridx = _iota((rows, 1), 0)
    tok = ridx & (SUB - 1)
    head = (ridx >> 3) & (R - 1)
    qpos = (past + tok).astype(F32)

    def make_q(g):
        pieces = []
        for m in range(2):
            for r in range(R):
                qs = _scaled_q(q_ref[:, g * 256 + r * 128:g * 256 + (r + 1) * 128], DF_HD ** -0.5)
                pieces.append(_half_mask(qs, lane, m == 1))
        slope = jnp.where(head == 0, sc_ref[2 + g * R], sc_ref[2 + g * R + 1])
        return jnp.concatenate(pieces, axis=0), slope

    def page_rows(p, g, j):
        return pages[p][pl.ds(2 * j + g, PAGE_SIZE, stride=2 * DF_KV), :].astype(BF16)

    for g in range(DF_KV):
        qg, slope = make_q(g)
        k_all = jnp.concatenate([page_rows(p, g, 0) for p in range(PP)], axis=0)
        v_all = jnp.concatenate([page_rows(p, g, 1) for p in range(PP)], axis=0)
        kpos = (c * (PP * PAGE_SIZE) + _iota((rows, PP * PAGE_SIZE), 1)).astype(F32)
        s = _dot_nt(qg, k_all) - slope * (qpos - kpos)
        m, l, acc = _softmax_update((m_s[g], l_s[g], acc_s[g]), s, lambda p: _dot(p, v_all))
        m_s[g], l_s[g], acc_s[g] = m, l, acc

    @pl.when(c == pl.num_programs(1) - 1)
    def _():
        lam, oml = sc_ref[0], sc_ref[1]
        pad = jnp.zeros((LANES - SUB, LANES), F32)
        jcol = _iota((rows, LANES), 1)
        for g in range(DF_KV):
            qg, slope = make_q(g)
            ks = g * 256
            knew = jnp.concatenate([new_ref[:, ks:ks + 128], pad], axis=0).astype(BF16)
            vnew = jnp.concatenate([new_ref[:, ks + 128:ks + 256], pad], axis=0).astype(BF16)
            dist = (tok - jcol).astype(F32)
            s = jnp.where((jcol <= tok) & (jcol < n_tok), _dot_nt(qg, knew) - slope * dist, NEG)
            _, l, acc = _softmax_update((m_s[g], l_s[g], acc_s[g]), s, lambda p: _dot(p, vnew))
            o = acc / l
            for r in range(R):
                o_r = jnp.concatenate([o[r * SUB:(r + 1) * SUB], o[(R + r) * SUB:(R + r + 1) * SUB]], axis=0)
                cs = slice(g * 256 + r * 128, g * 256 + (r + 1) * 128)
                a_ref[:, cs] = _diff_finish(o_r, lam, oml, nw_ref[...], z_ref[:, cs], SUB)


def _diff_sample(l, sc, pt_flat, ub, ukv, nw, cache, batch, n_pages, n_phys, n_tok):
    PP = min(16, n_pages)
    past = n_pages * PAGE_SIZE
    R = DF_HEADS // DF_KV
    rows = 2 * R * SUB

    def page_spec(p):
        return pl.BlockSpec((None, 4 * PAGE_SIZE, LANES),
                            lambda b, c, pt: (l * n_phys + pt[b * n_pages + c * PP + p], 0, 0))

    grid_spec = pltpu.PrefetchScalarGridSpec(
        num_scalar_prefetch=1,
        grid=(batch, n_pages // PP),
        in_specs=[_smem(),
                  pl.BlockSpec((SUB, 512), lambda b, c, pt: (b, UB_DFQ // 512)),
                  pl.BlockSpec((SUB, 512), lambda b, c, pt: (b, KV_DIFF // 512)),
                  pl.BlockSpec((SUB, 512), lambda b, c, pt: (b, UB_DFZ // 512)),
                  pl.BlockSpec((1, DF_VD), lambda b, c, pt: (0, 0))] + [page_spec(p) for p in range(PP)],
        out_specs=pl.BlockSpec((SUB, 512), lambda b, c, pt: (b, 0)),
        scratch_shapes=[pltpu.VMEM((DF_KV, rows, 1), F32), pltpu.VMEM((DF_KV, rows, 1), F32),
                        pltpu.VMEM((DF_KV, rows, DF_VD), F32)])
    return pl.pallas_call(
        functools.partial(_diff_sample_body, PP=PP, past=past, n_tok=n_tok),
        out_shape=jax.ShapeDtypeStruct((batch * SUB, 512), BF16),
        grid_spec=grid_spec,
        compiler_params=_params("arbitrary", "arbitrary"),
        name="diff_sample",
    )(pt_flat, sc, ub, ukv, ub, nw, *([cache] * PP))


def _block_sum(rows_ref, r0, cw):
    return jnp.sum(rows_ref[r0:r0 + CMP_BLK, 0:256] * cw, axis=0, keepdims=True)


def _top_blocks(imp, rowq, lane, n_blocks, n_pick):
    jq = rowq >> 6
    forced = (lane == 0) | (lane == jq) | (lane == jq - 1)
    score = jnp.where((lane <= jq) & (lane < n_blocks), imp + jnp.where(forced, FORCE, 0.0), NEG)
    lanef = lane.astype(F32)
    sel = jnp.zeros(imp.shape, F32)
    picks = []
    for _ in range(n_pick):
        mx = jnp.max(score, axis=-1, keepdims=True)
        am = jnp.min(jnp.where(score == mx, lanef, float(LANES)), axis=-1, keepdims=True)
        hit = lanef == am
        picked = mx > 0.5 * NEG
        sel = jnp.where(hit, jnp.where(picked, 1.0, 0.0), sel)
        score = jnp.where(hit, REMOVED, score)
        picks.append(jnp.where(picked, am, -1.0))
    return sel, picks


def _rank_select(score, n_blocks, n_pick):
    nv = score.shape[0] // SUB
    parts = [score[v * SUB:(v + 1) * SUB] for v in range(nv)]
    rank = [jnp.zeros(parts[0].shape, F32) for _ in range(nv)]
    rows = _iota(parts[0].shape, 0)
    for j in range(n_blocks):
        row = score[j:j + 1, :]
        for v in range(nv):
            if v * SUB > j:
                beats = jnp.where(row >= parts[v], 1.0, 0.0)
            elif v * SUB + SUB - 1 <= j:
                beats = jnp.where(row > parts[v], 1.0, 0.0)
            else:
                beats = jnp.where(rows + v * SUB > j, jnp.where(row >= parts[v], 1.0, 0.0),
                                  jnp.where(row > parts[v], 1.0, 0.0))
            rank[v] = rank[v] + beats
    return jnp.concatenate(
        [jnp.where((rank[v] < n_pick) & (parts[v] > 0.5 * NEG), 1.0, 0.0) for v in range(nv)], axis=0)


def _nsa_prompt_body(sl_ref, q_ref, gn_ref, z_ref, nsa_ref, win_ref, cw_ref, a_ref,
                     kce, kco, vce, vco, ksb, vst, kwb, vwt, *, tq, tk, seq):
    i = pl.program_id(1)
    nb = seq // SEL_BLK
    nbp = -(-nb // SUB) * SUB

    @pl.when(i == 0)
    def _():
        ksb[...] = nsa_ref[:, 256:384].astype(BF16)
        kwb[...] = win_ref[:, 0:128].astype(BF16)
        _fill_transposed(vst, nsa_ref, slice(384, 512), seq, tk)
        _fill_transposed(vwt, win_ref, slice(128, 256), seq, tk)
        kce[...] = jnp.zeros(kce.shape, F32)
        kco[...] = jnp.zeros(kco.shape, F32)
        cw = cw_ref[...]
        rows8 = _iota((SUB, 256), 0)
        v_even, v_odd = [], []
        for t8 in range(nbp // SUB):
            te = jnp.zeros((SUB, 256), F32)
            to = jnp.zeros((SUB, 256), F32)
            for e in range(SUB):
                j = t8 * SUB + e
                if j < nb:
                    te = jnp.where(rows8 == e, _block_sum(nsa_ref, 2 * j * CMP_BLK, cw), te)
                    to = jnp.where(rows8 == e, _block_sum(nsa_ref, (2 * j + 1) * CMP_BLK, cw), to)
            rs = slice(t8 * SUB, (t8 + 1) * SUB)
            kce[rs, :], kco[rs, :] = te[:, :128], to[:, :128]
            v_even.append(te[:, 128:])
            v_odd.append(to[:, 128:])
        pad = [jnp.zeros((LANES - nbp, LANES), F32)] if nbp < LANES else []
        vce[...] = jnp.concatenate(v_even + pad, axis=0).T
        vco[...] = jnp.concatenate(v_odd + pad, axis=0).T

    H = NS_KV * NS_R
    lane = _iota((tq, LANES), 1)
    blk = _iota((nbp, tq), 0)
    qpos = _iota((nbp, tq), 1) + i * tq
    qposf = qpos.astype(F32)
    kc = (kce[0:nbp, :].astype(BF16), kco[0:nbp, :].astype(BF16))
    vct = (vce[:, 0:nbp].astype(BF16), vco[:, 0:nbp].astype(BF16))
    cend = (blk * SEL_BLK + (CMP_BLK - 1), blk * SEL_BLK + (SEL_BLK - 1))
    valid = tuple((qpos >= ce) & (blk < nb) for ce in cend)
    qp = [_scaled_q(q_ref[:, r * 128:(r + 1) * 128], NS_HD ** -0.5 * LOG2E) for r in range(NS_R)]
    slopes = [sl_ref[h] * LOG2E for h in range(H)]
    qms = [_half_mask(qp[r], lane, g == 1) for g in range(NS_KV) for r in range(NS_R)]
    q8 = jnp.concatenate(qms, axis=0)
    srow8 = jnp.concatenate([jnp.full((1, tq), slopes[h], F32) for h in range(H)], axis=1)
    s = []
    for par in range(2):
        pen = jnp.concatenate([qposf - cend[par].astype(F32)] * H, axis=1) * srow8
        ok = jnp.concatenate([valid[par]] * H, axis=1)
        s.append((ok, jnp.where(ok, _dot_nt(kc[par], q8) - pen, NEG)))
    mx = jnp.maximum(jnp.max(s[0][1], axis=0, keepdims=True), jnp.max(s[1][1], axis=0, keepdims=True))
    e = [jnp.where(ok, jnp.exp2(sc - mx), 0.0) for ok, sc in s]
    den = jnp.maximum(jnp.sum(e[0], axis=0, keepdims=True) + jnp.sum(e[1], axis=0, keepdims=True), TINY)
    p = [e[0] / den, e[1] / den]
    o_cmp8 = _dot(vct[0], p[0].astype(BF16)) + _dot(vct[1], p[1].astype(BF16))
    o_cmp = [o_cmp8[:, h * tq:(h + 1) * tq] for h in range(H)]
    p_sum = p[0] + p[1]
    jq = qpos >> 6
    forced = (blk == 0) | (blk == jq) | (blk == jq - 1)
    sel_t = []
    for g in range(NS_KV):
        imp = sum(p_sum[:, (g * NS_R + r) * tq:(g * NS_R + r + 1) * tq] for r in range(NS_R))
        score = jnp.where((blk <= jq) & (blk < nb), imp + jnp.where(forced, FORCE, 0.0), NEG)
        sel_t.append(_rank_select(score, nb, SEL_N).astype(BF16))

    rel = _rel_t(tk, tq)
    nrel = [-slopes[h] * rel for h in range(H)]

    def scores(kt, k_ref, bias_fn):
        off = pl.multiple_of(kt * tk, tk)
        d0 = (i * tq - kt * tk).astype(F32)
        bias = bias_fn(off, rel + d0)
        s = _dot_nt(k_ref[pl.ds(off, tk), :], q8) + jnp.concatenate(
            [nrel[h] + bias[h // NS_R] for h in range(H)], axis=1)
        return s, -d0 * srow8

    def sel_bias(off, dist):
        expand = jnp.where(_iota((tk, nbp), 1) == ((_iota((tk, nbp), 0) + off) >> 6), 1.0, 0.0).astype(BF16)
        return [jnp.where((_dot(expand, st) > 0.5) & (dist >= 0), 0.0, NEG) for st in sel_t]

    def win_bias(off, dist):
        return [jnp.where((dist >= 0) & (dist <= WINDOW), 0.0, NEG)] * NS_KV

    n_t = ((i + 1) * tq + tk - 1) // tk
    _, l_sel, a_sel = _tflash_loop(0, n_t, lambda kt: scores(kt, ksb, sel_bias), lambda kt: vst[kt], H * tq)
    _, l_win, a_win = _tflash_loop(jnp.maximum(i * tq - WINDOW, 0) // tk, n_t,
                                   lambda kt: scores(kt, kwb, win_bias), lambda kt: vwt[kt], H * tq)
    o_sel, o_win = a_sel / l_sel, a_win / l_win
    gates = _sigmoid(gn_ref[...])
    gates_t = jnp.concatenate([gates[c:c + LANES, :].T for c in range(0, tq, LANES)], axis=1)
    half = _iota((LANES, tq), 0) < NS_HD
    for r in range(NS_R):
        og = []
        for g in range(NS_KV):
            h = g * NS_R + r
            cs = slice(h * tq, (h + 1) * tq)
            og.append(gates_t[3 * h:3 * h + 1, :] * o_cmp[h] + gates_t[3 * h + 1:3 * h + 2, :] * o_sel[:, cs]
                      + gates_t[3 * h + 2:3 * h + 3, :] * o_win[:, cs])
        o = _untranspose(jnp.where(half, og[0], og[1]), tq)
        cs = slice(r * 128, (r + 1) * 128)
        a_ref[:, cs] = (o * _silu(z_ref[:, cs].astype(F32))).astype(BF16)


def _nsa_prompt(sl, ub, uf, ukv, cw, batch, seq):
    tq = min(256, seq)
    tk = min(256, seq)
    nq = seq // tq
    row = lambda b, i: b * nq + i
    return pl.pallas_call(
        functools.partial(_nsa_prompt_body, tq=tq, tk=tk, seq=seq),
        out_shape=jax.ShapeDtypeStruct((batch * seq, 512), BF16),
        grid=(batch, nq),
        in_specs=[_smem(),
                  pl.BlockSpec((tq, 512), lambda b, i: (row(b, i), UB_NSQ // 512)),
                  pl.BlockSpec((tq, 128), lambda b, i: (row(b, i), UF_NSG // 128)),
                  pl.BlockSpec((tq, 512), lambda b, i: (row(b, i), UB_NSZ // 512)),
                  pl.BlockSpec((seq, 512), lambda b, i: (b, KV_NSA // 512)),
                  pl.BlockSpec((seq, 256), lambda b, i: (b, KV_WIN // 256)),
                  pl.BlockSpec((CMP_BLK, 256), lambda b, i: (0, 0))],
        out_specs=pl.BlockSpec((tq, 512), lambda b, i: (row(b, i), 0)),
        scratch_shapes=([pltpu.VMEM((LANES, 128), F32)] * 4
                        + [pltpu.VMEM((seq, 128), BF16), pltpu.VMEM((seq // tk, 128, tk), BF16)] * 2),
        compiler_params=_params("arbitrary", "arbitrary"),
        name="nsa_prompt",
    )(sl, ub, uf, ub, ukv, ukv, cw)


def _nsa_sample_a_body(pt_ref, sl_ref, q_ref, gn_ref, buf_ref, wnew_ref, wk_ref, wv_ref, *rest,
                       PP, past, n_tok, n_groups):
    del pt_ref
    pages = rest[:PP]
    part_ref, idx_ref = rest[PP:PP + 2]
    kct, vct = rest[PP + 2:]
    c = pl.program_id(1)
    nsb = past // SEL_BLK

    @pl.when(c == 0)
    def _():
        kct[...] = jnp.zeros(kct.shape, F32)
        vct[...] = jnp.zeros(vct.shape, F32)

    grp = (c * PP) // CMP_PAGES
    w0 = pl.multiple_of(((c * PP) % CMP_PAGES) * PAGE_SIZE, PP * PAGE_SIZE)
    k_all = jnp.concatenate([pages[p][0:128, :].astype(BF16) for p in range(PP)], axis=1)
    v_all = jnp.concatenate([pages[p][128:256, :].astype(BF16) for p in range(PP)], axis=1)
    kct[grp] = kct[grp] + _dot(k_all, wk_ref[pl.ds(w0, PP * PAGE_SIZE), :])
    vct[grp] = vct[grp] + _dot(v_all, wv_ref[pl.ds(w0, PP * PAGE_SIZE), :])

    @pl.when(c == pl.num_programs(1) - 1)
    def _():
        H = NS_KV * NS_R
        rows = H * SUB
        lane8 = _iota((SUB, LANES), 1)
        lane = _iota((rows, LANES), 1)
        tok = _iota((rows, LANES), 0) & (SUB - 1)
        rowq = tok + past
        qposf = rowq.astype(F32)
        qs, sl = [], []
        for g in range(NS_KV):
            for r in range(NS_R):
                qs.append(_half_mask(_scaled_q(q_ref[:, r * 128:(r + 1) * 128], NS_HD ** -0.5), lane8, g == 1))
                sl.append(jnp.full((SUB, 1), sl_ref[g * NS_R + r], F32))
        q8 = jnp.concatenate(qs, axis=0)
        slope = jnp.concatenate(sl, axis=0)
        s, valid = [], []
        for G in range(n_groups):
            blk = G * (LANES // 2) + (lane & (LANES // 2 - 1))
            cend = blk * SEL_BLK + (CMP_BLK - 1) + (lane >> 6) * CMP_BLK
            valid.append((rowq >= cend) & (blk < nsb))
            s.append(jnp.where(valid[G], _dot(q8, kct[G].astype(BF16)) - slope * (qposf - cend.astype(F32)), NEG))
        mx = functools.reduce(jnp.maximum, [jnp.max(x, axis=-1, keepdims=True) for x in s])
        e = [jnp.where(valid[G], jnp.exp(s[G] - mx), 0.0) for G in range(n_groups)]
        den = jnp.maximum(sum(jnp.sum(x, axis=-1, keepdims=True) for x in e), TINY)
        p = [x / den for x in e]
        o_cmp = sum(_dot_nt(p[G].astype(BF16), vct[G].astype(BF16)) for G in range(n_groups))
        imp = []
        for g in range(NS_KV):
            per_tile = []
            for G in range(n_groups):
                x = sum(p[G][(g * NS_R + r) * SUB:(g * NS_R + r + 1) * SUB] for r in range(NS_R))
                per_tile.append(x + pltpu.roll(x, LANES // 2, 1))
            imp.append(per_tile[0] if n_groups == 1 else jnp.where(lane8 < LANES // 2, per_tile[0], per_tile[1]))
        lane16 = _iota((NS_KV * SUB, LANES), 1)
        rowq16 = (_iota((NS_KV * SUB, LANES), 0) & (SUB - 1)) + past
        _, picks = _top_blocks(jnp.concatenate(imp, axis=0), rowq16, lane16, nsb, SEL_N - 1)
        idx = jnp.full((NS_KV * SUB, LANES), -1.0, F32)
        for kk, pk in enumerate(picks):
            idx = jnp.where(lane16 == kk, pk, idx)
        for g in range(NS_KV):
            idx_ref[g] = idx[g * SUB:(g + 1) * SUB].astype(jnp.int32)
        nbuf = buf_ref.shape[1]
        kw = buf_ref[0:128, :].astype(BF16)
        vw = buf_ref[128:256, :].astype(BF16)
        pad = jnp.zeros((LANES - SUB, LANES), F32)
        kwn = jnp.concatenate([wnew_ref[:, 0:128], pad], axis=0).astype(BF16)
        vwn = jnp.concatenate([wnew_ref[:, 128:256], pad], axis=0).astype(BF16)
        tokb = _iota((rows, nbuf), 0) & (SUB - 1)
        dist_b = tokb + (nbuf - _iota((rows, nbuf), 1))
        s_b = jnp.where((dist_b >= 0) & (dist_b <= WINDOW), _dot(q8, kw) - slope * dist_b.astype(F32), NEG)
        s_n = jnp.where((lane <= tok) & (lane < n_tok), _dot_nt(q8, kwn) - slope * (tok - lane).astype(F32), NEG)
        mx = jnp.maximum(jnp.max(s_b, axis=-1, keepdims=True), jnp.max(s_n, axis=-1, keepdims=True))
        p_b, p_n = jnp.exp(s_b - mx), jnp.exp(s_n - mx)
        den = jnp.sum(p_b, axis=-1, keepdims=True) + jnp.sum(p_n, axis=-1, keepdims=True)
        o_win = (_dot_nt(p_b.astype(BF16), vw) + _dot(p_n.astype(BF16), vwn)) / den
        gates = _sigmoid(gn_ref[...])
        for r in range(NS_R):
            og = []
            for g in range(NS_KV):
                h = g * NS_R + r
                hs = slice(h * SUB, (h + 1) * SUB)
                og.append(gates[:, 3 * h:3 * h + 1] * o_cmp[hs] + gates[:, 3 * h + 2:3 * h + 3] * o_win[hs])
            part_ref[:, r * 128:(r + 1) * 128] = jnp.where(lane8 < 64, og[0], og[1])


def _nsa_sample_a(l, sl, pt_flat, ub, uf, ukv, buf_t, wk, wv, cache_t, batch, n_pages, n_phys, n_tok):
    PP = min(16, n_pages)
    past = n_pages * PAGE_SIZE
    nbuf = buf_t.shape[2]
    n_groups = -(-n_pages // CMP_PAGES)
    assert CMP_PAGES % PP == 0

    def page_spec(p):
        return pl.BlockSpec((None, 256, PAGE_SIZE),
                            lambda b, c, pt: (l * n_phys + pt[b * n_pages + c * PP + p], 0, 0))

    grid_spec = pltpu.PrefetchScalarGridSpec(
        num_scalar_prefetch=1,
        grid=(batch, n_pages // PP),
        in_specs=[_smem(),
                  pl.BlockSpec((SUB, 512), lambda b, c, pt: (b, UB_NSQ // 512)),
                  pl.BlockSpec((SUB, 128), lambda b, c, pt: (b, UF_NSG // 128)),
                  pl.BlockSpec((None, 256, nbuf), lambda b, c, pt: (l * batch + b, 0, 0)),
                  pl.BlockSpec((SUB, 256), lambda b, c, pt: (b, KV_WIN // 256)),
                  pl.BlockSpec((CMP_PAGES * PAGE_SIZE, LANES), lambda b, c, pt: (0, 0)),
                  pl.BlockSpec((CMP_PAGES * PAGE_SIZE, LANES), lambda b, c, pt: (0, 0))]
                 + [page_spec(p) for p in range(PP)],
        out_specs=(pl.BlockSpec((SUB, 512), lambda b, c, pt: (b, 0)),
                   pl.BlockSpec((None, NS_KV, SUB, LANES), lambda b, c, pt: (b, 0, 0, 0))),
        scratch_shapes=[pltpu.VMEM((n_groups, LANES, LANES), F32)] * 2)
    return pl.pallas_call(
        functools.partial(_nsa_sample_a_body, PP=PP, past=past, n_tok=n_tok, n_groups=n_groups),
        out_shape=(jax.ShapeDtypeStruct((batch * SUB, 512), F32),
                   jax.ShapeDtypeStruct((batch, NS_KV, SUB, LANES), jnp.int32)),
        grid_spec=grid_spec,
        compiler_params=_params("arbitrary", "arbitrary"),
        name="nsa_sample_a",
    )(pt_flat, sl, ub, uf, buf_t, ukv, wk, wv, *([cache_t] * PP))


def _nsa_sample_b_body(pt_ref, idx_ref, sl_ref, q_ref, gn_ref, z_ref, new_ref, part_ref, *rest, NSEL, past, n_tok):
    del pt_ref
    TT = SUB // NS_R
    blocks = rest[:TT * NSEL]
    a_ref = rest[TT * NSEL]
    acc_scr = rest[TT * NSEL + 1]
    b, g, ts = pl.program_id(0), pl.program_id(1), pl.program_id(2)

    @pl.when((g == 0) & (ts == 0))
    def _():
        acc_scr[...] = part_ref[...]

    lane = _iota((SUB, LANES), 1)
    rows = _iota((SUB, LANES), 0)
    ghalf = (lane >> 6) == g
    rcol = _iota((SUB, 1), 0)
    sh = NS_R.bit_length() - 1
    tcol = ts * TT + (rcol >> sh)
    gates = _sigmoid(gn_ref[...])
    q = jnp.zeros((SUB, LANES), F32)
    slope = jnp.zeros((SUB, 1), F32)
    gate = jnp.zeros((SUB, 1), F32)
    for r in range(NS_R):
        qp = q_ref[:, r * 128:(r + 1) * 128].astype(F32)
        slope = jnp.where((rcol & (NS_R - 1)) == r, sl_ref[g * NS_R + r], slope)
        for tt in range(TT):
            t = ts * TT + tt
            q = jnp.where(rows == tt * NS_R + r, jnp.sum(jnp.where(rows == t, qp, 0.0), axis=0, keepdims=True), q)
            gv = jnp.where((rows == t) & (lane == (g * NS_R + r) * 3 + 1), gates, 0.0)
            gv = jnp.sum(jnp.sum(gv, axis=1, keepdims=True), axis=0, keepdims=True)
            gate = jnp.where(rcol == tt * NS_R + r, gv, gate)
    qb = jnp.where(ghalf, q * NS_HD ** -0.5, 0.0).astype(BF16)
    qpos = past + tcol
    key = ((rows >> sh) << 2) + (lane >> 6)
    bias = []
    for j in range(TT * NSEL):
        blk = idx_ref[((b * NS_KV + g) * n_tok + ts * TT + j // NSEL) * SEL_N + j % NSEL]
        half = jnp.where(blk >= 0, blk & 1, 2)
        kpos = (blk >> 1) * PAGE_SIZE + lane
        bias.append(jnp.where(key == ((j // NSEL) << 2) + half, -slope * (qpos - kpos).astype(F32), NEG))
    k_all = jnp.concatenate([blocks[j][0:128, :].astype(BF16) for j in range(TT * NSEL)], axis=1)
    v_all = jnp.concatenate([blocks[j][128:256, :].astype(BF16) for j in range(TT * NSEL)], axis=1)
    s = _dot(qb, k_all) + jnp.concatenate(bias, axis=1)
    pad = jnp.zeros((LANES - SUB, LANES), F32)
    knew = jnp.concatenate([new_ref[:, 0:128], pad], axis=0).astype(BF16)
    vnew = jnp.concatenate([new_ref[:, 128:256], pad], axis=0).astype(BF16)
    s_new = jnp.where((lane <= tcol) & (lane < n_tok), _dot_nt(qb, knew) - slope * (tcol - lane).astype(F32), NEG)
    mx = jnp.maximum(jnp.max(s, axis=-1, keepdims=True), jnp.max(s_new, axis=-1, keepdims=True))
    p, p_new = jnp.exp(s - mx), jnp.exp(s_new - mx)
    den = jnp.sum(p, axis=-1, keepdims=True) + jnp.sum(p_new, axis=-1, keepdims=True)
    o = gate * (_dot_nt(p.astype(BF16), v_all) + _dot(p_new.astype(BF16), vnew)) / den
    for r in range(NS_R):
        cs = slice(r * 128, (r + 1) * 128)
        upd = acc_scr[:, cs]
        for tt in range(TT):
            upd = upd + jnp.where((rows == ts * TT + tt) & ghalf, o[tt * NS_R + r:tt * NS_R + r + 1, :], 0.0)
        acc_scr[:, cs] = upd

    @pl.when((g == NS_KV - 1) & (ts == pl.num_programs(2) - 1))
    def _():
        a_ref[...] = (acc_scr[...] * _silu(z_ref[...].astype(F32))).astype(BF16)


def _nsa_sample_b(l, sl, pt_flat, idx_flat, ub, uf, ukv, part, cache_t, batch, n_pages, n_phys, n_tok):
    NSEL = SEL_N - 1
    past = n_pages * PAGE_SIZE
    halves = PAGE_SIZE // SEL_BLK

    TT = SUB // NS_R
    assert n_tok % TT == 0

    def blk_spec(j):
        def index(b, g, ts, pt, idx):
            blk = jnp.maximum(idx[((b * NS_KV + g) * n_tok + ts * TT + j // NSEL) * SEL_N + j % NSEL], 0)
            return (l * n_phys + pt[b * n_pages + blk // halves], 1, 0)
        return pl.BlockSpec((None, 256, PAGE_SIZE), index)

    grid_spec = pltpu.PrefetchScalarGridSpec(
        num_scalar_prefetch=2,
        grid=(batch, NS_KV, n_tok // TT),
        in_specs=[_smem(),
                  pl.BlockSpec((SUB, 512), lambda b, g, t, pt, idx: (b, UB_NSQ // 512)),
                  pl.BlockSpec((SUB, 128), lambda b, g, t, pt, idx: (b, UF_NSG // 128)),
                  pl.BlockSpec((SUB, 512), lambda b, g, t, pt, idx: (b, UB_NSZ // 512)),
                  pl.BlockSpec((SUB, 256), lambda b, g, t, pt, idx: (b, (KV_NSA + 256) // 256)),
                  pl.BlockSpec((SUB, 512), lambda b, g, t, pt, idx: (b, 0))]
                 + [blk_spec(j) for j in range(TT * NSEL)],
        out_specs=pl.BlockSpec((SUB, 512), lambda b, g, t, pt, idx: (b, 0)),
        scratch_shapes=[pltpu.VMEM((SUB, 512), F32)])
    return pl.pallas_call(
        functools.partial(_nsa_sample_b_body, NSEL=NSEL, past=past, n_tok=n_tok),
        out_shape=jax.ShapeDtypeStruct((batch * SUB, 512), BF16),
        grid_spec=grid_spec,
        compiler_params=_params("arbitrary", "arbitrary", "arbitrary"),
        name="nsa_sample_b",
    )(pt_flat, idx_flat, sl, ub, uf, ub, ukv, part, *([cache_t] * (TT * NSEL)))


def _final_body(x_ref, gate_ref, gp_ref, ahg_ref, adf_ref, ans_ref, mg_ref, whg_ref, wdf_ref, wns_ref, wo_ref, o_ref):
    d = x_ref.shape[-1]
    merged = (_sigmoid(mg_ref[:, 0:d].astype(F32)) * _dot(ahg_ref[...], whg_ref[...])
              + _sigmoid(mg_ref[:, d:2 * d].astype(F32)) * _dot(adf_ref[...], wdf_ref[...])
              + _sigmoid(mg_ref[:, 2 * d:3 * d].astype(F32)) * _dot(ans_ref[...], wns_ref[...]))
    out = _dot(merged.astype(BF16), wo_ref[...])
    ms = jnp.mean(out * out, axis=-1, keepdims=True)
    o_ref[...] = x_ref[...] + gate_ref[...] * (out * lax.rsqrt(ms + EPS) * gp_ref[...])


def _final(x2, cond, l, g_post, a_hg, a_df, a_ns, ub, w_hg, w_df, w_ns, w_o, tm, rows_per_batch, name):
    rows, d = x2.shape
    wspec = lambda k: pl.BlockSpec((None, k, d), lambda i: (l, 0, 0))
    return pl.pallas_call(
        _final_body,
        out_shape=jax.ShapeDtypeStruct((rows, d), F32),
        grid=(rows // tm,),
        in_specs=[pl.BlockSpec((tm, d), lambda i: (i, 0)),
                  _cond_spec(cond, l, 2, tm, rows_per_batch),
                  pl.BlockSpec((None, 1, d), lambda i: (l, 0, 0)),
                  pl.BlockSpec((tm, 512), lambda i: (i, 0)),
                  pl.BlockSpec((tm, 512), lambda i: (i, 0)),
                  pl.BlockSpec((tm, 512), lambda i: (i, 0)),
                  pl.BlockSpec((tm, 3 * d), lambda i: (i, UB_MG // (3 * d))),
                  wspec(512), wspec(512), wspec(512), wspec(d)],
        out_specs=pl.BlockSpec((tm, d), lambda i: (i, 0)),
        compiler_params=_params("arbitrary"),
        name=name,
    )(x2, cond, g_post, a_hg, a_df, a_ns, ub, w_hg, w_df, w_ns, w_o)


def _alibi(n):
    return [2.0 ** (-8.0 * (h + 1) / n) for h in range(n)]


def kernel(x_prompt, x_sample, cache_diff, cache_nsa, cache_nsa_win, state_hgrn, page_table, c_prompt, c_sample,
           w_cond, b_cond, g_pre, g_post, w_in, hg_lb, hg_norm, df_lam, df_norm, ns_cmp,
           w_hg_out, w_df_out, w_ns_out, w_out):
    depth, d, _ = w_in.shape
    bp, seq, _ = x_prompt.shape
    bs, n_tok, _ = x_sample.shape
    n_pages = page_table.shape[1]
    n_phys = cache_diff.shape[1]
    past = n_pages * PAGE_SIZE
    assert d == 1024 and seq % 128 == 0 and n_tok <= SUB and n_pages % 4 == 0
    assert cache_nsa_win.shape[2] == WINDOW and past >= WINDOW and past // SEL_BLK <= LANES

    kv_cols, uf_cols, ub_cols, pair = _column_layout()
    w_kv = jnp.take(w_in, kv_cols, axis=2).astype(BF16)
    w_uf = jnp.where(uf_cols >= 0, jnp.take(w_in, np.maximum(uf_cols, 0), axis=2), 0.0).astype(BF16)
    w_f32 = jnp.concatenate([w_kv, w_uf], axis=2)
    w_ub = jnp.take(w_in, ub_cols, axis=2).astype(BF16)
    w_hg = w_hg_out.astype(BF16)
    w_df = w_df_out.astype(BF16)
    w_ns = jnp.take(w_ns_out, pair, axis=1).astype(BF16)
    w_o = w_out.astype(BF16)
    g_pre3 = g_pre.reshape(depth, 1, d)
    g_post3 = g_post.reshape(depth, 1, d)
    lb_w = jax.nn.softmax(hg_lb.astype(F32), axis=0)
    lower = jnp.cumsum(lb_w, axis=0) - lb_w[0]
    cw_all = jax.nn.softmax(ns_cmp.astype(F32), axis=-1)
    cw_tile = jnp.repeat(jnp.swapaxes(cw_all, 1, 2), 128, axis=2)
    lv = df_lam.astype(F32)
    lam_init = jnp.asarray([0.8 - 0.6 * math.exp(-0.3 * l) for l in range(depth)], F32)
    lam = jnp.exp(jnp.sum(lv[:, 0] * lv[:, 1], axis=-1)) - jnp.exp(jnp.sum(lv[:, 2] * lv[:, 3], axis=-1)) + lam_init
    df_sc = jnp.concatenate([lam[:, None], 1.0 - lam_init[:, None],
                             jnp.broadcast_to(jnp.asarray(_alibi(DF_HEADS), F32), (depth, DF_HEADS))], axis=1)
    ns_sl = jnp.asarray(_alibi(NS_HEADS), F32)

    bc = bp + bs
    bc_pad = -(-bc // SUB) * SUB
    c_all = jnp.concatenate([c_prompt, c_sample, jnp.zeros((bc_pad - bc, d), F32)], axis=0)
    cond = _cond_all(c_all, w_cond, b_cond)
    cond_p = cond[:, :bp].reshape(depth, bp, 1, 3 * d)
    cond_s = jnp.repeat(cond[:, bp:bc], SUB, axis=1)

    pt_flat = page_table.reshape(-1).astype(jnp.int32)
    cache_diff2 = cache_diff.reshape(depth, n_phys, PAGE_SIZE, DF_KV, 2, LANES).transpose(0, 1, 2, 4, 3, 5)
    cache_diff2 = cache_diff2.reshape(depth * n_phys, 4 * PAGE_SIZE, LANES)
    cache_nsa_t = cache_nsa.transpose(0, 1, 3, 4, 5, 2).reshape(depth * n_phys, 512, PAGE_SIZE)
    win_buf_t = cache_nsa_win.transpose(0, 1, 3, 4, 5, 2).reshape(depth * bs, 256, WINDOW)
    pos = np.arange(PAGE_SIZE)
    nn = pos // CMP_BLK
    tgt = (nn % 2)[None, :] * (LANES // 2) + 2 * np.arange(CMP_PAGES)[:, None] + (nn // 2)[None, :]
    place = jnp.asarray(tgt[:, :, None] == np.arange(LANES)[None, None, :], F32)
    cw_pos = cw_all[:, :, pos % CMP_BLK]
    w_cmp = (place[None, None] * cw_pos[:, :, None, :, None]).astype(BF16)
    w_cmp = w_cmp.reshape(depth, 2, CMP_PAGES * PAGE_SIZE, LANES)

    xp = x_prompt.reshape(bp * seq, d)
    xs = jnp.pad(x_sample, ((0, 0), (0, SUB - n_tok), (0, 0))).reshape(bs * SUB, d)
    zeros_state = jnp.zeros((bp, HG_HEADS, HG_DK, HG_DV), F32)
    tm_p = min(1024, seq)
    tm_f = min(512, seq)
    rows_s = bs * SUB

    outs = {k: [] for k in ('dp', 'ds', 'np', 'ns', 'wp', 'ws', 'hp', 'hs')}
    for l in range(depth):
        lb = lower[l].reshape(1, 512)
        nw_hg = hg_norm[l].reshape(1, HG_DV)
        nw_df = df_norm[l].reshape(1, DF_VD)

        ukv, uf = _inproj_f32(xp, cond_p, l, g_pre3, w_f32, tm_p, seq, "inproj_f32_p")
        ub = _inproj(xp, cond_p, l, g_pre3, w_ub, UB_W // 2, BF16, tm_p, seq, "inproj_b_p")
        a_hg, s_p = _hgrn_prompt(ub, uf, lb, nw_hg, zeros_state, bp, seq)
        a_df = _diff_prompt(df_sc[l], ub, ukv, nw_df, bp, seq)
        a_ns = _nsa_prompt(ns_sl, ub, uf, ukv, cw_tile[l], bp, seq)
        xp = _final(xp, cond_p, l, g_post3, a_hg, a_df, a_ns, ub, w_hg, w_df, w_ns, w_o, tm_f, seq, "final_p")
        ukv3 = ukv.reshape(bp, seq, KV_W)
        outs['dp'].append(ukv3[:, :, KV_DIFF:KV_NSA].reshape(bp, seq, DF_KV, 256))
        outs['np'].append(ukv3[:, :, KV_NSA:KV_WIN].reshape(bp, seq, 4, NS_KV, NS_HD))
        outs['wp'].append(ukv3[:, seq - min(WINDOW, seq):, KV_WIN:].reshape(bp, min(WINDOW, seq), 2, NS_KV, NS_HD))
        outs['hp'].append(s_p)

        ukv, uf = _inproj_f32(xs, cond_s, l, g_pre3, w_f32, rows_s, SUB, "inproj_f32_s")
        ub = _inproj(xs, cond_s, l, g_pre3, w_ub, 512, BF16, rows_s, SUB, "inproj_b_s")
        a_hg, s_s = _hgrn_sample(ub, uf, lb, nw_hg, state_hgrn[l], bs, n_tok)
        a_df = _diff_sample(l, df_sc[l], pt_flat, ub, ukv, nw_df, cache_diff2, bs, n_pages, n_phys, n_tok)
        part, idx = _nsa_sample_a(l, ns_sl, pt_flat, ub, uf, ukv, win_buf_t, w_cmp[l, 0], w_cmp[l, 1], cache_nsa_t,
                                  bs, n_pages, n_phys, n_tok)
        idx_flat = idx[:, :, :n_tok, :SEL_N].reshape(-1)
        a_ns = _nsa_sample_b(l, ns_sl, pt_flat, idx_flat, ub, uf, ukv, part, cache_nsa_t,
                             bs, n_pages, n_phys, n_tok)
        xs = _final(xs, cond_s, l, g_post3, a_hg, a_df, a_ns, ub, w_hg, w_df, w_ns, w_o, rows_s, SUB, "final_s")
        ukv3 = ukv.reshape(bs, SUB, KV_W)[:, :n_tok]
        outs['ds'].append(ukv3[:, :, KV_DIFF:KV_NSA].reshape(bs, n_tok, DF_KV, 256))
        outs['ns'].append(ukv3[:, :, KV_NSA:KV_WIN].reshape(bs, n_tok, 4, NS_KV, NS_HD))
        new_w = ukv3[:, :, KV_WIN:].reshape(bs, n_tok, 2, NS_KV, NS_HD)
        outs['ws'].append(jnp.concatenate([cache_nsa_win[l][:, n_tok:], new_w], axis=1))
        outs['hs'].append(s_s)

    st = {k: jnp.stack(v) for k, v in outs.items()}
    y_p = xp.reshape(bp, seq, d)
    y_s = xs.reshape(bs, SUB, d)[:, :n_tok]
    return (y_p, y_s, st['dp'], st['ds'], st['np'], st['ns'], st['wp'], st['ws'], st['hp'], st['hs'])
```
